```python
import jax, jax.numpy as jnp
from jax import lax
import numpy as np

D_MODEL = 1024
BATCH = 32
SEQ = 2048
DEPTH = 2

N_MIXERS = 2
N_RET_LAYERS = (DEPTH + 1) // 2
N_CONV_LAYERS = DEPTH // 2

RET_HEADS = D_MODEL // 256
RET_QK_DIM = D_MODEL // RET_HEADS
RET_V_DIM = 2 * D_MODEL // RET_HEADS
RET_CHUNK = 128
ROPE_BASE = 10000.0

CONV_WIDTH = 31

FF_DIM = 7 * D_MODEL // 2
N_EXPERTS = 8
TOP_K = 2

EPS = 1e-6

kernel_name = "hybrid_retention_conformer_moe"


def rmsnorm(x, g):
    xf = x.astype(jnp.float32)
    y = xf * lax.rsqrt(jnp.mean(xf * xf, axis=-1, keepdims=True) + EPS) * g.astype(jnp.float32)
    return y.astype(x.dtype)


def rope(t, positions):
    half = t.shape[-1] // 2
    inv = ROPE_BASE ** (-jnp.arange(half, dtype=jnp.float32) / half)
    ang = positions.astype(jnp.float32)[..., None] * inv
    c = jnp.cos(ang)[:, :, None, :]
    s = jnp.sin(ang)[:, :, None, :]
    t1, t2 = t[..., :half], t[..., half:]
    return jnp.concatenate([t1 * c - t2 * s, t2 * c + t1 * s], axis=-1)


def chunkwise_retention(q, k, v):
    B, S, H, _ = q.shape
    C = RET_CHUNK
    N = S // C
    log_gamma = jnp.log1p(-(2.0 ** (-5.0 - jnp.arange(H, dtype=jnp.float32))))
    idx = jnp.arange(C, dtype=jnp.float32)
    rel = idx[:, None] - idx[None, :]
    decay = jnp.where(rel[None] >= 0,
                      jnp.exp(jnp.maximum(rel, 0.0)[None] * log_gamma[:, None, None]), 0.0)
    xi = jnp.exp((idx + 1.0)[None, :] * log_gamma[:, None])
    zeta = jnp.exp((C - 1.0 - idx)[None, :] * log_gamma[:, None])
    gamma_c = jnp.exp(C * log_gamma)

    def to_chunks(t):
        return t.reshape(B, N, C, H, t.shape[-1]).transpose(1, 0, 3, 2, 4)

    def step(state, inp):
        qc, kc, vc = inp
        scores = jnp.einsum('bhcd,bhmd->bhcm', qc, kc) * decay[None]
        intra = jnp.einsum('bhcm,bhmv->bhcv', scores, vc)
        cross = jnp.einsum('bhcd,bhdv->bhcv', qc, state) * xi[None, :, :, None]
        new_state = state * gamma_c[None, :, None, None] + jnp.einsum(
            'bhmd,bhmv->bhdv', kc * zeta[None, :, :, None], vc)
        return new_state, intra + cross

    state0 = jnp.zeros((B, H, q.shape[-1], v.shape[-1]), jnp.float32)
    _, out = lax.scan(step, state0, (to_chunks(q), to_chunks(k), to_chunks(v)))
    return out.transpose(1, 0, 3, 2, 4).reshape(B, S, H, v.shape[-1])


def retention_mixer(h, positions, w_in, w_out):
    B, S, _ = h.shape
    proj = h @ w_in
    qk_w = RET_HEADS * RET_QK_DIM
    v_w = RET_HEADS * RET_V_DIM
    q = proj[..., :qk_w]
    k = proj[..., qk_w:2 * qk_w]
    v = proj[..., 2 * qk_w:2 * qk_w + v_w]
    g = proj[..., 2 * qk_w + v_w:]
    q = rope(q.reshape(B, S, RET_HEADS, RET_QK_DIM).astype(jnp.float32), positions)
    k = rope(k.reshape(B, S, RET_HEADS, RET_QK_DIM).astype(jnp.float32), positions) * (RET_QK_DIM ** -0.5)
    v = v.reshape(B, S, RET_HEADS, RET_V_DIM).astype(jnp.float32)
    o = chunkwise_retention(q, k, v)
    mu = jnp.mean(o, axis=-1, keepdims=True)
    var = jnp.mean(jnp.square(o - mu), axis=-1, keepdims=True)
    o = ((o - mu) * lax.rsqrt(var + EPS)).reshape(B, S, v_w).astype(h.dtype)
    return (jax.nn.silu(g) * o) @ w_out


def conv_module(h, w_pw1, b_pw1, w_dw, b_dw, ln_g, ln_b, w_pw2, b_pw2):
    a = h @ w_pw1 + b_pw1
    u = a[..., :D_MODEL] * jax.nn.sigmoid(a[..., D_MODEL:])
    u = lax.conv_general_dilated(
        u, w_dw[:, None, :], window_strides=(1,), padding=[(CONV_WIDTH - 1, 0)],
        dimension_numbers=('NWC', 'WIO', 'NWC'), feature_group_count=D_MODEL) + b_dw
    uf = u.astype(jnp.float32)
    mu = jnp.mean(uf, axis=-1, keepdims=True)
    var = jnp.mean(jnp.square(uf - mu), axis=-1, keepdims=True)
    un = ((uf - mu) * lax.rsqrt(var + EPS) * ln_g.astype(jnp.float32) + ln_b.astype(jnp.float32)).astype(h.dtype)
    return jax.nn.silu(un) @ w_pw2 + b_pw2


def swiglu(h, w_gu, w_down):
    gu = h @ w_gu
    return (jax.nn.silu(gu[..., :FF_DIM]) * gu[..., FF_DIM:]) @ w_down


def moe_swiglu(h, w_router, w_gu, w_down):
    logits = (h @ w_router).astype(jnp.float32)
    top_vals, top_idx = lax.top_k(logits, TOP_K)
    gates = jax.nn.softmax(top_vals, axis=-1)
    comb = jnp.sum(jax.nn.one_hot(top_idx, N_EXPERTS, dtype=jnp.float32) * gates[..., None],
                   axis=-2).astype(h.dtype)
    y = jnp.zeros_like(h)
    for e in range(N_EXPERTS):
        y = y + comb[..., e:e + 1] * swiglu(h, w_gu[e], w_down[e])
    return y


def setup_inputs(seed: int = 0) -> dict:
    key = jax.random.key(seed)
    ks = jax.random.split(key, 20)
    f32 = jnp.float32
    D, F, E = D_MODEL, FF_DIM, N_EXPERTS
    ret_in = 2 * RET_HEADS * RET_QK_DIM + 2 * RET_HEADS * RET_V_DIM
    ret_v = RET_HEADS * RET_V_DIM

    def nrm(k, shape, fan_in):
        return jax.random.normal(k, shape, f32) * (fan_in ** -0.5)

    def gain(k, shape):
        return 1.0 + 0.02 * jax.random.normal(k, shape, f32)

    def bias(k, shape):
        return 0.02 * jax.random.normal(k, shape, f32)

    return {
        "x": jax.random.normal(ks[0], (BATCH, SEQ, D), f32),
        "positions": jnp.broadcast_to(jnp.arange(SEQ, dtype=jnp.int32)[None, :], (BATCH, SEQ)),
        "norm_mix_g": gain(ks[1], (DEPTH, D)),
        "norm_ffn_g": gain(ks[2], (DEPTH, D)),
        "final_norm_g": gain(ks[3], (D,)),
        "ret_w_in": nrm(ks[4], (N_RET_LAYERS, D, ret_in), D),
        "ret_w_out": nrm(ks[5], (N_RET_LAYERS, ret_v, D), ret_v),
        "conv_w_pw1": nrm(ks[6], (N_CONV_LAYERS, D, 2 * D), D),
        "conv_b_pw1": bias(ks[7], (N_CONV_LAYERS, 2 * D)),
        "conv_w_dw": nrm(ks[8], (N_CONV_LAYERS, CONV_WIDTH, D), CONV_WIDTH),
        "conv_b_dw": bias(ks[9], (N_CONV_LAYERS, D)),
        "conv_ln_g": gain(ks[10], (N_CONV_LAYERS, D)),
        "conv_ln_b": bias(ks[11], (N_CONV_LAYERS, D)),
        "conv_w_pw2": nrm(ks[12], (N_CONV_LAYERS, D, D), D),
        "conv_b_pw2": bias(ks[13], (N_CONV_LAYERS, D)),
        "ffn_w_gu": nrm(ks[14], (N_RET_LAYERS, D, 2 * F), D),
        "ffn_w_down": nrm(ks[15], (N_RET_LAYERS, F, D), F),
        "moe_w_router": nrm(ks[16], (N_CONV_LAYERS, D, E), D),
        "moe_w_gu": nrm(ks[17], (N_CONV_LAYERS, E, D, 2 * F), D),
        "moe_w_down": nrm(ks[18], (N_CONV_LAYERS, E, F, D), F),
    }


def reference(x, positions, norm_mix_g, norm_ffn_g, final_norm_g, ret_w_in, ret_w_out,
              conv_w_pw1, conv_b_pw1, conv_w_dw, conv_b_dw, conv_ln_g, conv_ln_b,
              conv_w_pw2, conv_b_pw2, ffn_w_gu, ffn_w_down, moe_w_router, moe_w_gu, moe_w_down):
    h = x
    for i in range(DEPTH):
        j = i // N_MIXERS
        hn = rmsnorm(h, norm_mix_g[i])
        if i % N_MIXERS == 0:
            h = h + retention_mixer(hn, positions, ret_w_in[j], ret_w_out[j])
        else:
            h = h + conv_module(hn, conv_w_pw1[j], conv_b_pw1[j], conv_w_dw[j], conv_b_dw[j],
                                conv_ln_g[j], conv_ln_b[j], conv_w_pw2[j], conv_b_pw2[j])
        hn = rmsnorm(h, norm_ffn_g[i])
        if i % 2 == 0:
            h = h + swiglu(hn, ffn_w_gu[j], ffn_w_down[j])
        else:
            h = h + moe_swiglu(hn, moe_w_router[j], moe_w_gu[j], moe_w_down[j])
    return rmsnorm(h, final_norm_g)
```

```python
import functools

import numpy as np
import jax
import jax.numpy as jnp
from jax import lax
from jax.experimental import pallas as pl
from jax.experimental.pallas import tpu as pltpu

F32 = jnp.float32
BF16 = jnp.bfloat16

EPS = 1e-6
ROPE_BASE = 10000.0
HEADS = 4
QK_DIM = 256
V_DIM = 512
CONV_WIDTH = 31
N_EXPERTS = 8
LANES = 128
HALO = 32
VMEM_LIMIT = 56 * 1024 * 1024

RET_CHUNK = 256


def _cparams(sem):
    return pltpu.CompilerParams(dimension_semantics=sem, vmem_limit_bytes=VMEM_LIMIT)


def _rms(h, g):
    return h * lax.rsqrt(jnp.mean(h * h, axis=-1, keepdims=True) + EPS) * g


def _inproj_kernel(h_ref, g_ref, pos_ref, inv_ref, w_ref, o_ref, hn_ref, cos_ref, sin_ref):
    j = pl.program_id(1)

    @pl.when(j == 0)
    def _():
        hn_ref[...] = _rms(h_ref[...], g_ref[...]).astype(BF16)
        ang = pos_ref[...].astype(F32) * inv_ref[...]
        cos_ref[...] = jnp.cos(ang)
        sin_ref[...] = jnp.sin(ang)

    acc = jnp.dot(hn_ref[...], w_ref[...], preferred_element_type=F32)

    @pl.when(j < 2)
    def _():
        scale = jnp.where(j == 1, QK_DIM ** -0.5, 1.0).astype(F32)
        c = cos_ref[...]
        s = sin_ref[...]
        half = QK_DIM // 2
        for hd in range(HEADS):
            t1 = acc[:, hd * QK_DIM: hd * QK_DIM + half]
            t2 = acc[:, hd * QK_DIM + half: (hd + 1) * QK_DIM]
            o_ref[:, hd * QK_DIM: hd * QK_DIM + half] = ((t1 * c - t2 * s) * scale).astype(BF16)
            o_ref[:, hd * QK_DIM + half: (hd + 1) * QK_DIM] = ((t2 * c + t1 * s) * scale).astype(BF16)

    @pl.when(j >= 2)
    def _():
        o_ref[...] = acc.astype(BF16)


def _inproj(h, g, posb, inv, w, tm=1024, tn=1024):
    T, D = h.shape
    N = w.shape[1]
    return pl.pallas_call(
        _inproj_kernel,
        grid=(T // tm, N // tn),
        in_specs=[
            pl.BlockSpec((tm, D), lambda i, j: (i, 0)),
            pl.BlockSpec((1, D), lambda i, j: (0, 0)),
            pl.BlockSpec((tm, LANES), lambda i, j: (i, 0)),
            pl.BlockSpec((1, LANES), lambda i, j: (0, 0)),
            pl.BlockSpec((D, tn), lambda i, j: (0, j)),
        ],
        out_specs=pl.BlockSpec((tm, tn), lambda i, j: (i, j)),
        out_shape=jax.ShapeDtypeStruct((T, N), BF16),
        scratch_shapes=[pltpu.VMEM((tm, D), BF16), pltpu.VMEM((tm, LANES), F32),
                        pltpu.VMEM((tm, LANES), F32)],
        compiler_params=_cparams(("parallel", "arbitrary")),
        name="ret_inproj",
    )(h, g, posb, inv, w)


def _retention_kernel(gamma_c, q_ref, k_ref, v_ref, g_ref, dec_ref, xi_ref, zeta_ref, o_ref, st_ref):
    n = pl.program_id(1)

    @pl.when(n == 0)
    def _():
        st_ref[...] = jnp.zeros_like(st_ref)

    for h in range(HEADS):
        q = q_ref[:, h * QK_DIM:(h + 1) * QK_DIM]
        k = k_ref[:, h * QK_DIM:(h + 1) * QK_DIM]
        v = v_ref[:, h * V_DIM:(h + 1) * V_DIM]
        s = lax.dot_general(q, k, (((1,), (1,)), ((), ())), preferred_element_type=F32)
        s = s * dec_ref[h]
        intra = jnp.dot(s.astype(BF16), v, preferred_element_type=F32)
        state = st_ref[h]
        cross = jnp.dot(q, state.astype(BF16), preferred_element_type=F32) * xi_ref[h]
        o = intra + cross
        kz = (k.astype(F32) * zeta_ref[h]).astype(BF16)
        st_ref[h] = state * gamma_c[h] + lax.dot_general(
            kz, v, (((0,), (0,)), ((), ())), preferred_element_type=F32)
        mu = jnp.mean(o, axis=-1, keepdims=True)
        d = o - mu
        var = jnp.mean(d * d, axis=-1, keepdims=True)
        on = d * lax.rsqrt(var + EPS)
        gate = g_ref[:, h * V_DIM:(h + 1) * V_DIM].astype(F32)
        o_ref[:, h * V_DIM:(h + 1) * V_DIM] = (gate * jax.nn.sigmoid(gate) * on).astype(BF16)


def _retention_consts(C):
    log_gamma = np.log1p(-(2.0 ** (-5.0 - np.arange(HEADS, dtype=np.float64))))
    idx = np.arange(C, dtype=np.float64)
    rel = idx[:, None] - idx[None, :]
    decay = np.where(rel[None] >= 0, np.exp(np.maximum(rel, 0.0)[None] * log_gamma[:, None, None]), 0.0)
    xi = np.exp((idx + 1.0)[None, :] * log_gamma[:, None])
    zeta = np.exp((C - 1.0 - idx)[None, :] * log_gamma[:, None])
    gamma_c = tuple(float(x) for x in np.exp(C * log_gamma))
    xi_b = np.broadcast_to(xi[:, :, None], (HEADS, C, V_DIM))
    zeta_b = np.broadcast_to(zeta[:, :, None], (HEADS, C, QK_DIM))
    return (gamma_c, jnp.asarray(decay, F32), jnp.asarray(xi_b, F32), jnp.asarray(zeta_b, F32))


def _retention(proj, B, S):
    T = proj.shape[0]
    C = RET_CHUNK
    N = S // C
    qk_w = HEADS * QK_DIM
    v_w = HEADS * V_DIM
    gamma_c, decay, xi_b, zeta_b = _retention_consts(C)
    row = lambda b, n: b * N + n
    return pl.pallas_call(
        functools.partial(_retention_kernel, gamma_c),
        grid=(B, N),
        in_specs=[
            pl.BlockSpec((C, qk_w), lambda b, n: (row(b, n), 0)),
            pl.BlockSpec((C, qk_w), lambda b, n: (row(b, n), 1)),
            pl.BlockSpec((C, v_w), lambda b, n: (row(b, n), 1)),
            pl.BlockSpec((C, v_w), lambda b, n: (row(b, n), 2)),
            pl.BlockSpec((HEADS, C, C), lambda b, n: (0, 0, 0)),
            pl.BlockSpec((HEADS, C, V_DIM), lambda b, n: (0, 0, 0)),
            pl.BlockSpec((HEADS, C, QK_DIM), lambda b, n: (0, 0, 0)),
        ],
        out_specs=pl.BlockSpec((C, v_w), lambda b, n: (row(b, n), 0)),
        out_shape=jax.ShapeDtypeStruct((T, v_w), BF16),
        scratch_shapes=[pltpu.VMEM((HEADS, QK_DIM, V_DIM), F32)],
        compiler_params=_cparams(("parallel", "arbitrary")),
        name="retention",
    )(proj, proj, proj, proj, decay, xi_b, zeta_b)


def _matmul_res_kernel(x_ref, w_ref, r_ref, o_ref):
    o_ref[...] = r_ref[...] + jnp.dot(x_ref[...], w_ref[...], preferred_element_type=F32)


def _matmul_res(x, w, res, tm=512):
    T, K = x.shape
    D = w.shape[1]
    return pl.pallas_call(
        _matmul_res_kernel,
        grid=(T // tm,),
        in_specs=[
            pl.BlockSpec((tm, K), lambda i: (i, 0)),
            pl.BlockSpec((K, D), lambda i: (0, 0)),
            pl.BlockSpec((tm, D), lambda i: (i, 0)),
        ],
        out_specs=pl.BlockSpec((tm, D), lambda i: (i, 0)),
        out_shape=jax.ShapeDtypeStruct((T, D), F32),
        compiler_params=_cparams(("parallel",)),
        name="ret_outproj",
    )(x, w, res)


def _ffn_kernel(h_ref, g_ref, wg_ref, wu_ref, wd_ref, o_ref, hn_ref, acc_ref):
    j = pl.program_id(1)

    @pl.when(j == 0)
    def _():
        hn_ref[...] = _rms(h_ref[...], g_ref[...]).astype(BF16)
        acc_ref[...] = jnp.zeros_like(acc_ref)

    hn = hn_ref[...]
    gate = jnp.dot(hn, wg_ref[...], preferred_element_type=F32)
    up = jnp.dot(hn, wu_ref[...], preferred_element_type=F32)
    act = (gate * jax.nn.sigmoid(gate) * up).astype(BF16)
    acc_ref[...] += jnp.dot(act, wd_ref[...], preferred_element_type=F32)

    @pl.when(j == pl.num_programs(1) - 1)
    def _():
        o_ref[...] = h_ref[...] + acc_ref[...]


def _ffn(h, g, w_gu, w_down, tm=512, tf=512):
    T, D = h.shape
    F = w_down.shape[0]
    nf = F // tf
    return pl.pallas_call(
        _ffn_kernel,
        grid=(T // tm, nf),
        in_specs=[
            pl.BlockSpec((tm, D), lambda i, j: (i, 0)),
            pl.BlockSpec((1, D), lambda i, j: (0, 0)),
            pl.BlockSpec((D, tf), lambda i, j: (0, j)),
            pl.BlockSpec((D, tf), lambda i, j: (0, j + nf)),
            pl.BlockSpec((tf, D), lambda i, j: (j, 0)),
        ],
        out_specs=pl.BlockSpec((tm, D), lambda i, j: (i, 0)),
        out_shape=jax.ShapeDtypeStruct((T, D), F32),
        scratch_shapes=[pltpu.VMEM((tm, D), BF16), pltpu.VMEM((tm, D), F32)],
        compiler_params=_cparams(("parallel", "arbitrary")),
        name="dense_swiglu",
    )(h, g, w_gu, w_gu, w_down)


def _conv_kernel(h_ref, g_ref, w1_ref, b1_ref, wdw_ref, bdw_ref, lng_ref, lnb_ref, w2_ref, b2_ref,
                 o_ref, u_ref, c_ref):
    s = pl.program_id(1)
    ts, D = h_ref.shape

    @pl.when(s == 0)
    def _():
        u_ref[0:HALO, :] = jnp.zeros((HALO, D), F32)

    @pl.when(s > 0)
    def _():
        u_ref[0:HALO, :] = u_ref[ts:ts + HALO, :]

    h = h_ref[...]
    hn = _rms(h, g_ref[...]).astype(BF16)
    a = jnp.dot(hn, w1_ref[...], preferred_element_type=F32) + b1_ref[...]
    u_ref[HALO:HALO + ts, :] = a[:, :D] * jax.nn.sigmoid(a[:, D:])

    rc = 32
    base = HALO - (CONV_WIDTH - 1)

    for r in range(ts // rc):
        r0 = r * rc
        acc = jnp.zeros((rc, D), F32)
        for j in range(CONV_WIDTH):
            acc = acc + u_ref[r0 + base + j:r0 + base + j + rc, :] * wdw_ref[j:j + 1, :]
        c_ref[r0:r0 + rc, :] = acc

    c = c_ref[...] + bdw_ref[...]
    mu = jnp.mean(c, axis=-1, keepdims=True)
    d = c - mu
    var = jnp.mean(d * d, axis=-1, keepdims=True)
    un = d * lax.rsqrt(var + EPS) * lng_ref[...] + lnb_ref[...]
    act = (un * jax.nn.sigmoid(un)).astype(BF16)
    o_ref[...] = h + jnp.dot(act, w2_ref[...], preferred_element_type=F32) + b2_ref[...]


def _conv_module(h, g, w1, b1, wdw, bdw, lng, lnb, w2, b2, B, S, ts=512):
    T, D = h.shape
    ns = S // ts
    W = wdw.shape[0]
    row = lambda b, s: (b * ns + s, 0)
    const = lambda b, s: (0, 0)
    return pl.pallas_call(
        _conv_kernel,
        grid=(B, ns),
        in_specs=[
            pl.BlockSpec((ts, D), row),
            pl.BlockSpec((1, D), const),
            pl.BlockSpec((D, 2 * D), const),
            pl.BlockSpec((1, 2 * D), const),
            pl.BlockSpec((W, D), const),
            pl.BlockSpec((1, D), const),
            pl.BlockSpec((1, D), const),
            pl.BlockSpec((1, D), const),
            pl.BlockSpec((D, D), const),
            pl.BlockSpec((1, D), const),
        ],
        out_specs=pl.BlockSpec((ts, D), row),
        out_shape=jax.ShapeDtypeStruct((T, D), F32),
        scratch_shapes=[pltpu.VMEM((ts + HALO, D), F32), pltpu.VMEM((ts, D), F32)],
        compiler_params=_cparams(("parallel", "arbitrary")),
        name="conv_module",
    )(h, g, w1, b1, wdw, bdw, lng, lnb, w2, b2)


def _router_kernel(h_ref, g_ref, wr_ref, hn_ref, meta_ref, cnt_ref, run_ref):
    i = pl.program_id(0)
    tm = h_ref.shape[0]

    @pl.when(i == 0)
    def _():
        run_ref[...] = jnp.zeros_like(run_ref)

    hn = _rms(h_ref[...], g_ref[...])
    hn_ref[...] = hn
    logits = jnp.dot(hn, wr_ref[...], preferred_element_type=F32, precision=lax.Precision.HIGHEST)
    lane = lax.broadcasted_iota(jnp.int32, logits.shape, 1)
    neg = jnp.float32(-jnp.inf)
    logits = jnp.where(lane < N_EXPERTS, logits, neg)
    m1 = jnp.max(logits, axis=-1, keepdims=True)
    i1 = jnp.min(jnp.where(logits == m1, lane, LANES), axis=-1, keepdims=True)
    rest = jnp.where(lane == i1, neg, logits)
    m2 = jnp.max(rest, axis=-1, keepdims=True)
    i2 = jnp.min(jnp.where(rest == m2, lane, LANES), axis=-1, keepdims=True)
    e = jnp.exp(m2 - m1)
    g1 = 1.0 / (1.0 + e)
    g2 = e / (1.0 + e)

    oh1 = (lane == i1).astype(F32)
    oh2 = (lane == i2).astype(F32)
    sel = oh1 + oh2
    r_i = lax.broadcasted_iota(jnp.int32, (tm, tm), 0)
    c_i = lax.broadcasted_iota(jnp.int32, (tm, tm), 1)
    tri = (c_i < r_i).astype(BF16)
    prefix = jnp.dot(tri, sel.astype(BF16), preferred_element_type=F32) + run_ref[...]
    rank1 = jnp.sum(oh1 * prefix, axis=-1, keepdims=True)
    rank2 = jnp.sum(oh2 * prefix, axis=-1, keepdims=True)
    run_ref[...] = run_ref[...] + jnp.sum(sel, axis=0, keepdims=True)
    cnt_ref[...] = jnp.broadcast_to(run_ref[...], cnt_ref.shape)

    meta = jnp.where(lane == 0, i1.astype(F32), 0.0)
    meta = jnp.where(lane == 1, i2.astype(F32), meta)
    meta = jnp.where(lane == 2, rank1, meta)
    meta = jnp.where(lane == 3, rank2, meta)
    meta = jnp.where(lane == 4, g1, meta)
    meta = jnp.where(lane == 5, g2, meta)
    meta_ref[...] = meta


def _router(h, g, wr_pad, tm=512):
    T, D = h.shape
    return pl.pallas_call(
        _router_kernel,
        grid=(T // tm,),
        in_specs=[
            pl.BlockSpec((tm, D), lambda i: (i, 0)),
            pl.BlockSpec((1, D), lambda i: (0, 0)),
            pl.BlockSpec((D, LANES), lambda i: (0, 0)),
        ],
        out_specs=[
            pl.BlockSpec((tm, D), lambda i: (i, 0)),
            pl.BlockSpec((tm, LANES), lambda i: (i, 0)),
            pl.BlockSpec((8, LANES), lambda i: (0, 0)),
        ],
        out_shape=[
            jax.ShapeDtypeStruct((T, D), F32),
            jax.ShapeDtypeStruct((T, LANES), F32),
            jax.ShapeDtypeStruct((8, LANES), F32),
        ],
        scratch_shapes=[pltpu.VMEM((1, LANES), F32)],
        compiler_params=_cparams(("arbitrary",)),
        name="moe_router",
    )(h, g, wr_pad)


def _row_copy(src_ref, dst_ref, sem, src_row, dst_row):
    return pltpu.make_async_copy(src_ref.at[pl.ds(src_row, 1), :], dst_ref.at[pl.ds(dst_row, 1), :], sem)


def _dispatch_kernel(p1_ref, p2_ref, hn_ref, xs_in_ref, xs_ref, sem):
    del xs_in_ref
    i = pl.program_id(0)
    tb = p1_ref.shape[0]

    def start(r, c):
        t = i * tb + r
        _row_copy(hn_ref, xs_ref, sem, t, p1_ref[r]).start()
        _row_copy(hn_ref, xs_ref, sem, t, p2_ref[r]).start()
        return c

    lax.fori_loop(0, tb, start, 0)

    def wait(r, c):
        _row_copy(hn_ref, xs_ref, sem, 0, 0).wait()
        _row_copy(hn_ref, xs_ref, sem, 0, 0).wait()
        return c

    lax.fori_loop(0, tb, wait, 0)


def _dispatch(hn, pos1, pos2, xs_init, tb=512):
    T, D = hn.shape
    return pl.pallas_call(
        _dispatch_kernel,
        grid=(T // tb,),
        in_specs=[
            pl.BlockSpec((tb,), lambda i: (i,), memory_space=pltpu.SMEM),
            pl.BlockSpec((tb,), lambda i: (i,), memory_space=pltpu.SMEM),
            pl.BlockSpec(memory_space=pl.ANY),
            pl.BlockSpec(memory_space=pl.ANY),
        ],
        out_specs=pl.BlockSpec(memory_space=pl.ANY),
        out_shape=jax.ShapeDtypeStruct(xs_init.shape, xs_init.dtype),
        scratch_shapes=[pltpu.SemaphoreType.DMA(())],
        input_output_aliases={3: 0},
        compiler_params=_cparams(("arbitrary",)),
        name="moe_dispatch",
    )(pos1, pos2, hn, xs_init)


def _moe_ffn_kernel(te_ref, tv_ref, x_ref, wg_ref, wu_ref, wd_ref, o_ref, xb_ref, acc_ref):
    i = pl.program_id(0)
    j = pl.program_id(1)

    @pl.when(tv_ref[i] > 0)
    def _():
        @pl.when(j == 0)
        def _():
            xb_ref[...] = x_ref[...].astype(BF16)
            acc_ref[...] = jnp.zeros_like(acc_ref)

        x = xb_ref[...]
        gate = jnp.dot(x, wg_ref[0], preferred_element_type=F32)
        up = jnp.dot(x, wu_ref[0], preferred_element_type=F32)
        act = (gate * jax.nn.sigmoid(gate) * up).astype(BF16)
        acc_ref[...] += jnp.dot(act, wd_ref[0], preferred_element_type=F32)

        @pl.when(j == pl.num_programs(1) - 1)
        def _():
            o_ref[...] = acc_ref[...]

    @pl.when(jnp.logical_and(tv_ref[i] == 0, j == pl.num_programs(1) - 1))
    def _():
        o_ref[...] = jnp.zeros_like(o_ref)


def _moe_ffn(tile_expert, tile_valid, xs, w_gu, w_down, tm, tf=512):
    R, D = xs.shape
    F = w_down.shape[1]
    nf = F // tf
    grid_spec = pltpu.PrefetchScalarGridSpec(
        num_scalar_prefetch=2,
        grid=(R // tm, nf),
        in_specs=[
            pl.BlockSpec((tm, D), lambda i, j, te, tv: (i, 0)),
            pl.BlockSpec((1, D, tf), lambda i, j, te, tv: (te[i], 0, j)),
            pl.BlockSpec((1, D, tf), lambda i, j, te, tv: (te[i], 0, j + nf)),
            pl.BlockSpec((1, tf, D), lambda i, j, te, tv: (te[i], j, 0)),
        ],
        out_specs=pl.BlockSpec((tm, D), lambda i, j, te, tv: (i, 0)),
        scratch_shapes=[pltpu.VMEM((tm, D), BF16), pltpu.VMEM((tm, D), F32)],
    )
    return pl.pallas_call(
        _moe_ffn_kernel,
        grid_spec=grid_spec,
        out_shape=jax.ShapeDtypeStruct((R, D), F32),
        compiler_params=_cparams(("arbitrary", "arbitrary")),
        name="moe_swiglu",
    )(tile_expert, tile_valid, xs, w_gu, w_gu, w_down)


def _combine_kernel(p1_ref, p2_ref, h_ref, meta_ref, g_ref, ys_ref, o_ref, y1_ref, y2_ref, sem):
    tb = h_ref.shape[0]

    def start(r, c):
        _row_copy(ys_ref, y1_ref, sem, p1_ref[r], r).start()
        _row_copy(ys_ref, y2_ref, sem, p2_ref[r], r).start()
        return c

    lax.fori_loop(0, tb, start, 0)

    def wait(r, c):
        _row_copy(ys_ref, y1_ref, sem, 0, 0).wait()
        _row_copy(ys_ref, y2_ref, sem, 0, 0).wait()
        return c

    lax.fori_loop(0, tb, wait, 0)

    meta = meta_ref[...]
    g1 = meta[:, 4:5]
    g2 = meta[:, 5:6]
    h = h_ref[...] + g1 * y1_ref[...] + g2 * y2_ref[...]
    o_ref[...] = _rms(h, g_ref[...])


def _combine(h, meta, g, ys, pos1, pos2, tb=512):
    T, D = h.shape
    return pl.pallas_call(
        _combine_kernel,
        grid=(T // tb,),
        in_specs=[
            pl.BlockSpec((tb,), lambda i: (i,), memory_space=pltpu.SMEM),
            pl.BlockSpec((tb,), lambda i: (i,), memory_space=pltpu.SMEM),
            pl.BlockSpec((tb, D), lambda i: (i, 0)),
            pl.BlockSpec((tb, LANES), lambda i: (i, 0)),
            pl.BlockSpec((1, D), lambda i: (0, 0)),
            pl.BlockSpec(memory_space=pl.ANY),
        ],
        out_specs=pl.BlockSpec((tb, D), lambda i: (i, 0)),
        out_shape=jax.ShapeDtypeStruct((T, D), F32),
        scratch_shapes=[pltpu.VMEM((tb, D), F32), pltpu.VMEM((tb, D), F32),
                        pltpu.SemaphoreType.DMA(())],
        compiler_params=_cparams(("arbitrary",)),
        name="moe_combine",
    )(pos1, pos2, h, meta, g, ys)


def _moe_layer(h, g_ffn, g_final, w_router, w_gu, w_down, tm=512):
    T, D = h.shape
    E = w_router.shape[1]
    wr_pad = jnp.zeros((D, LANES), F32).at[:, :E].set(w_router)
    hn, meta, cnt = _router(h, g_ffn, wr_pad)

    counts = cnt[0, :E].astype(jnp.int32)
    padded = ((counts + tm - 1) // tm) * tm
    ends = jnp.cumsum(padded)
    offsets = ends - padded
    e1 = meta[:, 0].astype(jnp.int32)
    e2 = meta[:, 1].astype(jnp.int32)
    pos1 = offsets[e1] + meta[:, 2].astype(jnp.int32)
    pos2 = offsets[e2] + meta[:, 3].astype(jnp.int32)
    n_tiles = (2 * T) // tm + E
    tile_start = jnp.arange(n_tiles, dtype=jnp.int32) * tm
    tile_expert = jnp.minimum(jnp.searchsorted(ends, tile_start, side="right"), E - 1).astype(jnp.int32)
    tile_valid = (tile_start < ends[-1]).astype(jnp.int32)

    xs = _dispatch(hn, pos1, pos2, jnp.zeros((n_tiles * tm, D), F32))
    ys = _moe_ffn(tile_expert, tile_valid, xs, w_gu, w_down, tm)
    return _combine(h, meta, g_final, ys, pos1, pos2)


def kernel(x, positions, norm_mix_g, norm_ffn_g, final_norm_g, ret_w_in, ret_w_out, conv_w_pw1, conv_b_pw1, conv_w_dw, conv_b_dw, conv_ln_g, conv_ln_b, conv_w_pw2, conv_b_pw2, ffn_w_gu, ffn_w_down, moe_w_router, moe_w_gu, moe_w_down):
    B, S, D = x.shape
    T = B * S
    h = x.reshape(T, D)
    row = lambda v: v.reshape(1, -1)

    posb = jnp.broadcast_to(positions.reshape(T, 1), (T, LANES))
    half = QK_DIM // 2
    inv = (ROPE_BASE ** (-jnp.arange(half, dtype=F32) / half)).reshape(1, half)
    proj = _inproj(h, row(norm_mix_g[0]), posb, inv, ret_w_in[0].astype(BF16))
    gated = _retention(proj, B, S)
    h = _matmul_res(gated, ret_w_out[0].astype(BF16), h)
    h = _ffn(h, row(norm_ffn_g[0]), ffn_w_gu[0].astype(BF16), ffn_w_down[0].astype(BF16))
    h = _conv_module(h, row(norm_mix_g[1]), conv_w_pw1[0].astype(BF16), row(conv_b_pw1[0]),
                     conv_w_dw[0], row(conv_b_dw[0]), row(conv_ln_g[0]), row(conv_ln_b[0]),
                     conv_w_pw2[0].astype(BF16), row(conv_b_pw2[0]), B, S)
    out = _moe_layer(h, row(norm_ffn_g[1]), row(final_norm_g), moe_w_router[0],
                     moe_w_gu[0].astype(BF16), moe_w_down[0].astype(BF16))
    return out.reshape(B, S, D)
```

```python
import functools

import numpy as np
import jax
import jax.numpy as jnp
from jax import lax
from jax.experimental import pallas as pl
from jax.experimental.pallas import tpu as pltpu

F32 = jnp.float32
BF16 = jnp.bfloat16

EPS = 1e-6
ROPE_BASE = 10000.0
HEADS = 4
QK_DIM = 256
V_DIM = 512
CONV_WIDTH = 31
N_EXPERTS = 8
LANES = 128
SUBLANES = 8
HALO = 32
VMEM_LIMIT = 56 * 1024 * 1024

RET_CHUNK = 256
FF_CHUNK = 512
MOE_TILE = 896
SMEM_BLOCK = 1024


def _cparams(sem):
    return pltpu.CompilerParams(dimension_semantics=sem, vmem_limit_bytes=VMEM_LIMIT)


def _resident(shape):
    zeros = (0,) * len(shape)
    return pl.BlockSpec(shape, lambda *_: zeros, pipeline_mode=pl.Buffered(1))


def _rms(h, g):
    return h * lax.rsqrt(jnp.mean(h * h, axis=-1, keepdims=True) + EPS) * g


def _silu(x):
    return x * jax.nn.sigmoid(x)


def _inproj_kernel(h_ref, g_ref, pos_ref, inv_ref, w_ref, o_ref):
    hn = _rms(h_ref[...], g_ref[...]).astype(BF16)
    ang = pos_ref[...].astype(F32) * inv_ref[...]
    c = jnp.cos(ang)
    s = jnp.sin(ang)
    half = QK_DIM // 2
    qk_w = HEADS * QK_DIM
    for hd in range(2 * HEADS):
        col = hd * QK_DIM
        a = jnp.dot(hn, w_ref[:, col:col + QK_DIM], preferred_element_type=F32)
        t1 = a[:, :half]
        t2 = a[:, half:]
        scale = 1.0 if hd < HEADS else QK_DIM ** -0.5
        o_ref[:, col:col + half] = ((t1 * c - t2 * s) * scale).astype(BF16)
        o_ref[:, col + half:col + QK_DIM] = ((t2 * c + t1 * s) * scale).astype(BF16)
    blk = 1024
    for col in range(2 * qk_w, w_ref.shape[1], blk):
        o_ref[:, col:col + blk] = jnp.dot(hn, w_ref[:, col:col + blk], preferred_element_type=F32).astype(BF16)


def _inproj(h, g, posb, inv, w, tm=512):
    T, D = h.shape
    N = w.shape[1]
    return pl.pallas_call(
        _inproj_kernel,
        grid=(T // tm,),
        in_specs=[
            pl.BlockSpec((tm, D), lambda i: (i, 0)),
            _resident((1, D)),
            pl.BlockSpec((tm, LANES), lambda i: (i, 0)),
            _resident((1, LANES)),
            _resident((D, N)),
        ],
        out_specs=pl.BlockSpec((tm, N), lambda i: (i, 0)),
        out_shape=jax.ShapeDtypeStruct((T, N), BF16),
        compiler_params=_cparams(("parallel",)),
        name="ret_inproj",
    )(h, g, posb, inv, w)


def _retention_kernel(gamma_c, q_ref, k_ref, v_ref, g_ref, dec_ref, xi_ref, zeta_ref, wo_ref, h_ref,
                      o_ref, st_ref, gated_ref):
    n = pl.program_id(1)

    @pl.when(n == 0)
    def _():
        st_ref[...] = jnp.zeros_like(st_ref)

    for h in range(HEADS):
        q = q_ref[:, h * QK_DIM:(h + 1) * QK_DIM]
        k = k_ref[:, h * QK_DIM:(h + 1) * QK_DIM]
        v = v_ref[:, h * V_DIM:(h + 1) * V_DIM]
        s = lax.dot_general(q, k, (((1,), (1,)), ((), ())), preferred_element_type=F32)
        s = s * dec_ref[h]
        intra = jnp.dot(s.astype(BF16), v, preferred_element_type=F32)
        state = st_ref[h]
        cross = jnp.dot(q, state.astype(BF16), preferred_element_type=F32) * xi_ref[h]
        o = intra + cross
        kz = (k.astype(F32) * zeta_ref[h]).astype(BF16)
        st_ref[h] = state * gamma_c[h] + lax.dot_general(
            kz, v, (((0,), (0,)), ((), ())), preferred_element_type=F32)
        mu = jnp.mean(o, axis=-1, keepdims=True)
        d = o - mu
        var = jnp.mean(d * d, axis=-1, keepdims=True)
        on = d * lax.rsqrt(var + EPS)
        gate = g_ref[:, h * V_DIM:(h + 1) * V_DIM].astype(F32)
        gated_ref[:, h * V_DIM:(h + 1) * V_DIM] = (_silu(gate) * on).astype(BF16)

    o_ref[...] = h_ref[...] + jnp.dot(gated_ref[...], wo_ref[...], preferred_element_type=F32)


def _retention_consts(C):
    log_gamma = np.log1p(-(2.0 ** (-5.0 - np.arange(HEADS, dtype=np.float64))))
    idx = np.arange(C, dtype=np.float64)
    rel = idx[:, None] - idx[None, :]
    decay = np.where(rel[None] >= 0, np.exp(np.maximum(rel, 0.0)[None] * log_gamma[:, None, None]), 0.0)
    xi = np.exp((idx + 1.0)[None, :] * log_gamma[:, None])
    zeta = np.exp((C - 1.0 - idx)[None, :] * log_gamma[:, None])
    gamma_c = tuple(float(x) for x in np.exp(C * log_gamma))
    xi_b = np.broadcast_to(xi[:, :, None], (HEADS, C, V_DIM))
    zeta_b = np.broadcast_to(zeta[:, :, None], (HEADS, C, QK_DIM))
    return (gamma_c, jnp.asarray(decay, F32), jnp.asarray(xi_b, F32), jnp.asarray(zeta_b, F32))


def _retention(proj, w_out, h, B, S):
    T, D = h.shape
    C = RET_CHUNK
    N = S // C
    qk_w = HEADS * QK_DIM
    v_w = HEADS * V_DIM
    gamma_c, decay, xi_b, zeta_b = _retention_consts(C)
    row = lambda b, n: b * N + n
    return pl.pallas_call(
        functools.partial(_retention_kernel, gamma_c),
        grid=(B, N),
        in_specs=[
            pl.BlockSpec((C, qk_w), lambda b, n: (row(b, n), 0)),
            pl.BlockSpec((C, qk_w), lambda b, n: (row(b, n), 1)),
            pl.BlockSpec((C, v_w), lambda b, n: (row(b, n), 1)),
            pl.BlockSpec((C, v_w), lambda b, n: (row(b, n), 2)),
            _resident((HEADS, C, C)),
            _resident((HEADS, C, V_DIM)),
            _resident((HEADS, C, QK_DIM)),
            _resident((v_w, D)),
            pl.BlockSpec((C, D), lambda b, n: (row(b, n), 0)),
        ],
        out_specs=pl.BlockSpec((C, D), lambda b, n: (row(b, n), 0)),
        out_shape=jax.ShapeDtypeStruct((T, D), F32),
        scratch_shapes=[pltpu.VMEM((HEADS, QK_DIM, V_DIM), F32), pltpu.VMEM((C, v_w), BF16)],
        compiler_params=_cparams(("parallel", "arbitrary")),
        name="retention",
    )(proj, proj, proj, proj, decay, xi_b, zeta_b, w_out, h)


def _ffn_kernel(h_ref, g_ref, wgu_ref, wd_ref, o_ref):
    h = h_ref[...]
    hn = _rms(h, g_ref[...]).astype(BF16)
    F = wd_ref.shape[0]
    acc = jnp.zeros(h.shape, F32)
    for c0 in range(0, F, FF_CHUNK):
        gate = jnp.dot(hn, wgu_ref[:, c0:c0 + FF_CHUNK], preferred_element_type=F32)
        up = jnp.dot(hn, wgu_ref[:, F + c0:F + c0 + FF_CHUNK], preferred_element_type=F32)
        act = (_silu(gate) * up).astype(BF16)
        acc = acc + jnp.dot(act, wd_ref[c0:c0 + FF_CHUNK, :], preferred_element_type=F32)
    o_ref[...] = h + acc


def _ffn(h, g, w_gu, w_down, tm=512):
    T, D = h.shape
    F = w_down.shape[0]
    return pl.pallas_call(
        _ffn_kernel,
        grid=(T // tm,),
        in_specs=[
            pl.BlockSpec((tm, D), lambda i: (i, 0)),
            _resident((1, D)),
            _resident((D, 2 * F)),
            _resident((F, D)),
        ],
        out_specs=pl.BlockSpec((tm, D), lambda i: (i, 0)),
        out_shape=jax.ShapeDtypeStruct((T, D), F32),
        compiler_params=_cparams(("parallel",)),
        name="dense_swiglu",
    )(h, g, w_gu, w_down)


def _conv_kernel(h_ref, g_ref, w1_ref, b1_ref, wdw_ref, bdw_ref, lng_ref, lnb_ref, w2_ref, b2_ref,
                 o_ref, u_ref, us_ref, c_ref):
    s = pl.program_id(1)
    ts, D = h_ref.shape

    @pl.when(s == 0)
    def _():
        u_ref[0:HALO, :] = jnp.zeros((HALO, D), F32)

    @pl.when(s > 0)
    def _():
        u_ref[0:HALO, :] = u_ref[ts:ts + HALO, :]

    h = h_ref[...]
    hn = _rms(h, g_ref[...]).astype(BF16)
    a = jnp.dot(hn, w1_ref[...], preferred_element_type=F32) + b1_ref[...]
    u_ref[HALO:HALO + ts, :] = a[:, :D] * jax.nn.sigmoid(a[:, D:])

    L = ts + HALO - SUBLANES
    for k in range(1, SUBLANES):
        us_ref[k - 1, 0:L, :] = u_ref[k:k + L, :]

    rc = 32
    base = HALO - (CONV_WIDTH - 1)

    def chunk(r, carry):
        r0 = pl.multiple_of(r * rc, rc)
        acc = jnp.zeros((rc, D), F32)
        for j in range(CONV_WIDTH):
            al, ph = divmod(base + j, SUBLANES)
            rows = pl.ds(r0 + al * SUBLANES, rc)
            tap = u_ref[rows, :] if ph == 0 else us_ref[ph - 1, rows, :]
            acc = acc + tap * wdw_ref[j:j + 1, :]
        c_ref[pl.ds(r0, rc), :] = acc
        return carry

    lax.fori_loop(0, ts // rc, chunk, 0)

    c = c_ref[...] + bdw_ref[...]
    mu = jnp.mean(c, axis=-1, keepdims=True)
    d = c - mu
    var = jnp.mean(d * d, axis=-1, keepdims=True)
    un = d * lax.rsqrt(var + EPS) * lng_ref[...] + lnb_ref[...]
    act = _silu(un).astype(BF16)
    o_ref[...] = h + jnp.dot(act, w2_ref[...], preferred_element_type=F32) + b2_ref[...]


def _conv_module(h, g, w1, b1, wdw, bdw, lng, lnb, w2, b2, B, S, ts=512):
    T, D = h.shape
    ns = S // ts
    W = wdw.shape[0]
    row = lambda b, s: (b * ns + s, 0)
    return pl.pallas_call(
        _conv_kernel,
        grid=(B, ns),
        in_specs=[
            pl.BlockSpec((ts, D), row),
            _resident((1, D)),
            _resident((D, 2 * D)),
            _resident((1, 2 * D)),
            _resident((W, D)),
            _resident((1, D)),
            _resident((1, D)),
            _resident((1, D)),
            _resident((D, D)),
            _resident((1, D)),
        ],
        out_specs=pl.BlockSpec((ts, D), row),
        out_shape=jax.ShapeDtypeStruct((T, D), F32),
        scratch_shapes=[pltpu.VMEM((ts + HALO, D), F32), pltpu.VMEM((SUBLANES - 1, ts + HALO, D), F32),
                        pltpu.VMEM((ts, D), F32)],
        compiler_params=_cparams(("parallel", "arbitrary")),
        name="conv_module",
    )(h, g, w1, b1, wdw, bdw, lng, lnb, w2, b2)


def _router_kernel(h_ref, g_ref, wr_ref, hn_ref, meta_ref, cnt_ref, run_ref):
    i = pl.program_id(0)
    tm = h_ref.shape[0]

    @pl.when(i == 0)
    def _():
        run_ref[...] = jnp.zeros_like(run_ref)

    hn = _rms(h_ref[...], g_ref[...])
    hn_ref[...] = hn
    logits = jnp.dot(hn, wr_ref[...], preferred_element_type=F32, precision=lax.Precision.HIGHEST)
    lane = lax.broadcasted_iota(jnp.int32, logits.shape, 1)
    neg = jnp.float32(-jnp.inf)
    logits = jnp.where(lane < N_EXPERTS, logits, neg)
    m1 = jnp.max(logits, axis=-1, keepdims=True)
    i1 = jnp.min(jnp.where(logits == m1, lane, LANES), axis=-1, keepdims=True)
    rest = jnp.where(lane == i1, neg, logits)
    m2 = jnp.max(rest, axis=-1, keepdims=True)
    i2 = jnp.min(jnp.where(rest == m2, lane, LANES), axis=-1, keepdims=True)
    e = jnp.exp(m2 - m1)
    g1 = 1.0 / (1.0 + e)
    g2 = e / (1.0 + e)

    oh1 = (lane == i1).astype(F32)
    oh2 = (lane == i2).astype(F32)
    sel = oh1 + oh2
    r_i = lax.broadcasted_iota(jnp.int32, (tm, tm), 0)
    c_i = lax.broadcasted_iota(jnp.int32, (tm, tm), 1)
    tri = (c_i < r_i).astype(BF16)
    prefix = jnp.dot(tri, sel.astype(BF16), preferred_element_type=F32) + run_ref[...]
    rank1 = jnp.sum(oh1 * prefix, axis=-1, keepdims=True)
    rank2 = jnp.sum(oh2 * prefix, axis=-1, keepdims=True)
    run_ref[...] = run_ref[...] + jnp.sum(sel, axis=0, keepdims=True)
    cnt_ref[...] = jnp.broadcast_to(run_ref[...], cnt_ref.shape)

    meta = jnp.where(lane == 0, i1.astype(F32), 0.0)
    meta = jnp.where(lane == 1, i2.astype(F32), meta)
    meta = jnp.where(lane == 2, rank1, meta)
    meta = jnp.where(lane == 3, rank2, meta)
    meta = jnp.where(lane == 4, g1, meta)
    meta = jnp.where(lane == 5, g2, meta)
    meta_ref[...] = meta


def _router(h, g, wr_pad, tm=512):
    T, D = h.shape
    return pl.pallas_call(
        _router_kernel,
        grid=(T // tm,),
        in_specs=[
            pl.BlockSpec((tm, D), lambda i: (i, 0)),
            _resident((1, D)),
            _resident((D, LANES)),
        ],
        out_specs=[
            pl.BlockSpec((tm, D), lambda i: (i, 0)),
            pl.BlockSpec((tm, LANES), lambda i: (i, 0)),
            pl.BlockSpec((8, LANES), lambda i: (0, 0)),
        ],
        out_shape=[
            jax.ShapeDtypeStruct((T, D), F32),
            jax.ShapeDtypeStruct((T, LANES), F32),
            jax.ShapeDtypeStruct((8, LANES), F32),
        ],
        scratch_shapes=[pltpu.VMEM((1, LANES), F32)],
        compiler_params=_cparams(("arbitrary",)),
        name="moe_router",
    )(h, g, wr_pad)


def _moe_ffn_kernel(te_ref, nv_ref, gsrc_ref, sdst_ref, hn_hbm, wg_ref, wu_ref, wd_ref, y_hbm,
                    xbuf, xb, acc, ystage, gsem, ssem):
    del te_ref
    i = pl.program_id(0)
    j = pl.program_id(1)
    ni = pl.num_programs(0)
    nf = pl.num_programs(1)
    tm = xb.shape[0]
    rpj = tm // nf
    gslot = lax.rem(i, 2)
    cslot = 1 - gslot
    sslot = gslot
    nv = nv_ref[i]

    def gather_all(slot):
        return pltpu.make_async_copy(hn_hbm.at[pl.ds(0, tm), :], xbuf.at[slot], gsem.at[slot])

    def scatter_wait(slot, n):
        n8 = pl.multiple_of(lax.shift_left(lax.shift_right_logical(n, 3), 3), SUBLANES)

        @pl.when(n8 > 0)
        def _():
            pltpu.make_async_copy(ystage.at[slot, pl.ds(0, n8), :], y_hbm.at[pl.ds(0, n8), :],
                                  ssem.at[slot]).wait()

        def one(k, carry):
            pltpu.make_async_copy(ystage.at[slot, pl.ds(0, 1), :], y_hbm.at[pl.ds(0, 1), :],
                                  ssem.at[slot]).wait()
            return carry

        lax.fori_loop(0, n - n8, one, 0)

    @pl.when(j == 0)
    def _():
        @pl.when(i == 0)
        def _():
            xb[...] = jnp.zeros_like(xb)

        @pl.when(i > 0)
        def _():
            gather_all(cslot).wait()
            xb[...] = xbuf[cslot].astype(BF16)

        acc[...] = jnp.zeros_like(acc)

    for u in range(rpj):
        r = j * rpj + u
        pltpu.make_async_copy(hn_hbm.at[pl.ds(gsrc_ref[r], 1), :], xbuf.at[gslot, pl.ds(r, 1), :],
                              gsem.at[gslot]).start()

        @pl.when(r < nv)
        def _():
            pltpu.make_async_copy(ystage.at[sslot, pl.ds(r, 1), :], y_hbm.at[pl.ds(sdst_ref[r], 1), :],
                                  ssem.at[sslot]).start()

    x = xb[...]
    gate = jnp.dot(x, wg_ref[0], preferred_element_type=F32)
    up = jnp.dot(x, wu_ref[0], preferred_element_type=F32)
    act = (_silu(gate) * up).astype(BF16)
    acc[...] += jnp.dot(act, wd_ref[0], preferred_element_type=F32)

    @pl.when(j == nf - 1)
    def _():
        @pl.when(i > 0)
        def _():
            scatter_wait(cslot, nv_ref[i - 1])

        ystage[cslot] = acc[...]

        @pl.when(i == ni - 1)
        def _():
            gather_all(gslot).wait()
            scatter_wait(sslot, nv)


def _moe_ffn(te_steps, nv_steps, gsrc, sdst, hn, w_gu, w_down, tm, tf=FF_CHUNK):
    T, D = hn.shape
    F = w_down.shape[1]
    nf = F // tf
    n_steps = gsrc.shape[0] // SMEM_BLOCK
    grid_spec = pltpu.PrefetchScalarGridSpec(
        num_scalar_prefetch=2,
        grid=(n_steps, nf),
        in_specs=[
            pl.BlockSpec((SMEM_BLOCK,), lambda i, j, te, nv: (i,), memory_space=pltpu.SMEM),
            pl.BlockSpec((SMEM_BLOCK,), lambda i, j, te, nv: (i,), memory_space=pltpu.SMEM),
            pl.BlockSpec(memory_space=pl.ANY),
            pl.BlockSpec((1, D, tf), lambda i, j, te, nv: (te[i], 0, j)),
            pl.BlockSpec((1, D, tf), lambda i, j, te, nv: (te[i], 0, j + nf)),
            pl.BlockSpec((1, tf, D), lambda i, j, te, nv: (te[i], j, 0)),
        ],
        out_specs=pl.BlockSpec(memory_space=pl.ANY),
        scratch_shapes=[
            pltpu.VMEM((2, tm, D), F32),
            pltpu.VMEM((tm, D), BF16),
            pltpu.VMEM((tm, D), F32),
            pltpu.VMEM((2, tm, D), F32),
            pltpu.SemaphoreType.DMA((2,)),
            pltpu.SemaphoreType.DMA((2,)),
        ],
    )
    return pl.pallas_call(
        _moe_ffn_kernel,
        grid_spec=grid_spec,
        out_shape=jax.ShapeDtypeStruct((2 * T, D), F32),
        compiler_params=_cparams(("arbitrary", "arbitrary")),
        name="moe_swiglu",
    )(te_steps, nv_steps, gsrc, sdst, hn, w_gu, w_gu, w_down)


def _combine_kernel(h_ref, meta_ref, g_ref, y1_ref, y2_ref, o_ref):
    meta = meta_ref[...]
    h = h_ref[...] + meta[:, 4:5] * y1_ref[...] + meta[:, 5:6] * y2_ref[...]
    o_ref[...] = _rms(h, g_ref[...])


def _combine(h, meta, g, y, tb=512):
    T, D = h.shape
    nb = T // tb
    return pl.pallas_call(
        _combine_kernel,
        grid=(nb,),
        in_specs=[
            pl.BlockSpec((tb, D), lambda i: (i, 0)),
            pl.BlockSpec((tb, LANES), lambda i: (i, 0)),
            _resident((1, D)),
            pl.BlockSpec((tb, D), lambda i: (i, 0)),
            pl.BlockSpec((tb, D), lambda i: (i + nb, 0)),
        ],
        out_specs=pl.BlockSpec((tb, D), lambda i: (i, 0)),
        out_shape=jax.ShapeDtypeStruct((T, D), F32),
        compiler_params=_cparams(("parallel",)),
        name="moe_combine",
    )(h, meta, g, y, y)


def _moe_layer(h, g_ffn, g_final, w_router, w_gu, w_down, tm=MOE_TILE):
    T, D = h.shape
    E = w_router.shape[1]
    wr_pad = jnp.zeros((D, LANES), F32).at[:, :E].set(w_router)
    hn, meta, cnt = _router(h, g_ffn, wr_pad)

    counts = cnt[0, :E].astype(jnp.int32)
    padded = ((counts + tm - 1) // tm) * tm
    ends = jnp.cumsum(padded)
    offsets = ends - padded
    e1 = meta[:, 0].astype(jnp.int32)
    e2 = meta[:, 1].astype(jnp.int32)
    pos1 = offsets[e1] + meta[:, 2].astype(jnp.int32)
    pos2 = offsets[e2] + meta[:, 3].astype(jnp.int32)
    n_tiles = -(-(2 * T + E * (tm - 1)) // tm)
    R = n_tiles * tm
    n_steps = n_tiles + 2
    tok = jnp.arange(T, dtype=jnp.int32)
    pos = jnp.concatenate([pos1, pos2])
    src = jnp.zeros((R,), jnp.int32).at[pos].set(jnp.concatenate([tok, tok]))
    dst = jnp.zeros((R,), jnp.int32).at[pos].set(jnp.concatenate([tok, T + tok]))
    gsrc = jnp.concatenate([src, jnp.zeros((2 * tm,), jnp.int32)])
    sdst = jnp.concatenate([jnp.zeros((2 * tm,), jnp.int32), dst])
    step_block = lambda t: jnp.pad(t.reshape(n_steps, tm), ((0, 0), (0, SMEM_BLOCK - tm))).reshape(-1)
    gsrc, sdst = step_block(gsrc), step_block(sdst)
    tile_start = jnp.arange(n_tiles, dtype=jnp.int32) * tm
    tile_expert = jnp.minimum(jnp.searchsorted(ends, tile_start, side="right"), E - 1).astype(jnp.int32)
    te_steps = tile_expert[jnp.clip(jnp.arange(n_steps) - 1, 0, n_tiles - 1)]
    tile_rows = jnp.clip((offsets + counts)[tile_expert] - tile_start, 0, tm).astype(jnp.int32)
    nv_steps = jnp.concatenate([jnp.zeros((2,), jnp.int32), tile_rows])

    y = _moe_ffn(te_steps, nv_steps, gsrc, sdst, hn, w_gu, w_down, tm)
    return _combine(h, meta, g_final, y)


def kernel(x, positions, norm_mix_g, norm_ffn_g, final_norm_g, ret_w_in, ret_w_out, conv_w_pw1, conv_b_pw1, conv_w_dw, conv_b_dw, conv_ln_g, conv_ln_b, conv_w_pw2, conv_b_pw2, ffn_w_gu, ffn_w_down, moe_w_router, moe_w_gu, moe_w_down):
    B, S, D = x.shape
    T = B * S
    h = x.reshape(T, D)
    row = lambda v: v.reshape(1, -1)

    posb = jnp.broadcast_to(positions.reshape(T, 1), (T, LANES))
    half = QK_DIM // 2
    inv = (ROPE_BASE ** (-jnp.arange(half, dtype=F32) / half)).reshape(1, half)
    proj = _inproj(h, row(norm_mix_g[0]), posb, inv, ret_w_in[0].astype(BF16))
    h = _retention(proj, ret_w_out[0].astype(BF16), h, B, S)
    h = _ffn(h, row(norm_ffn_g[0]), ffn_w_gu[0].astype(BF16), ffn_w_down[0].astype(BF16))
    h = _conv_module(h, row(norm_mix_g[1]), conv_w_pw1[0].astype(BF16), row(conv_b_pw1[0]),
                     conv_w_dw[0], row(conv_b_dw[0]), row(conv_ln_g[0]), row(conv_ln_b[0]),
                     conv_w_pw2[0].astype(BF16), row(conv_b_pw2[0]), B, S)
    out = _moe_layer(h, row(norm_ffn_g[1]), row(final_norm_g), moe_w_router[0],
                     moe_w_gu[0].astype(BF16), moe_w_down[0].astype(BF16))
    return out.reshape(B, S, D)
```

```python
import functools

import numpy as np
import jax
import jax.numpy as jnp
from jax import lax
from jax.experimental import pallas as pl
from jax.experimental.pallas import tpu as pltpu

F32 = jnp.float32
BF16 = jnp.bfloat16

EPS = 1e-6
ROPE_BASE = 10000.0
HEADS = 4
QK_DIM = 256
V_DIM = 512
CONV_WIDTH = 31
N_EXPERTS = 8
LANES = 128
SUBLANES = 8
HALO = 32
VMEM_LIMIT = 56 * 1024 * 1024

RET_CHUNK = 256
FF_CHUNK = 512
MOE_TILE = 896
SMEM_BLOCK = 1024


def _cparams(sem):
    return pltpu.CompilerParams(dimension_semantics=sem, vmem_limit_bytes=VMEM_LIMIT)


def _resident(shape):
    zeros = (0,) * len(shape)
    return pl.BlockSpec(shape, lambda *_: zeros, pipeline_mode=pl.Buffered(1))


def _rms(h, g):
    return h * lax.rsqrt(jnp.mean(h * h, axis=-1, keepdims=True) + EPS) * g


def _silu(x):
    return x * jax.nn.sigmoid(x)


def _inproj_kernel(h_ref, g_ref, pos_ref, inv_ref, w_ref, o_ref):
    hn = _rms(h_ref[...], g_ref[...]).astype(BF16)
    ang = pos_ref[...].astype(F32) * inv_ref[...]
    c = jnp.cos(ang)
    s = jnp.sin(ang)
    half = QK_DIM // 2
    qk_w = HEADS * QK_DIM
    for hd in range(2 * HEADS):
        col = hd * QK_DIM
        a = jnp.dot(hn, w_ref[:, col:col + QK_DIM], preferred_element_type=F32)
        t1 = a[:, :half]
        t2 = a[:, half:]
        scale = 1.0 if hd < HEADS else QK_DIM ** -0.5
        o_ref[:, col:col + half] = ((t1 * c - t2 * s) * scale).astype(BF16)
        o_ref[:, col + half:col + QK_DIM] = ((t2 * c + t1 * s) * scale).astype(BF16)
    blk = 1024
    for col in range(2 * qk_w, w_ref.shape[1], blk):
        o_ref[:, col:col + blk] = jnp.dot(hn, w_ref[:, col:col + blk], preferred_element_type=F32).astype(BF16)


def _inproj(h, g, posb, inv, w, tm=512):
    T, D = h.shape
    N = w.shape[1]
    return pl.pallas_call(
        _inproj_kernel,
        grid=(T // tm,),
        in_specs=[
            pl.BlockSpec((tm, D), lambda i: (i, 0)),
            _resident((1, D)),
            pl.BlockSpec((tm, LANES), lambda i: (i, 0)),
            _resident((1, LANES)),
            _resident((D, N)),
        ],
        out_specs=pl.BlockSpec((tm, N), lambda i: (i, 0)),
        out_shape=jax.ShapeDtypeStruct((T, N), BF16),
        compiler_params=_cparams(("parallel",)),
        name="ret_inproj",
    )(h, g, posb, inv, w)


def _retention_kernel(gamma_c, q_ref, k_ref, v_ref, g_ref, dec_ref, xi_ref, zeta_ref, wo_ref, h_ref,
                      o_ref, st_ref, gated_ref):
    n = pl.program_id(1)

    @pl.when(n == 0)
    def _():
        st_ref[...] = jnp.zeros_like(st_ref)

    for h in range(HEADS):
        q = q_ref[:, h * QK_DIM:(h + 1) * QK_DIM]
        k = k_ref[:, h * QK_DIM:(h + 1) * QK_DIM]
        v = v_ref[:, h * V_DIM:(h + 1) * V_DIM]
        s = lax.dot_general(q, k, (((1,), (1,)), ((), ())), preferred_element_type=F32)
        s = s * dec_ref[h]
        intra = jnp.dot(s.astype(BF16), v, preferred_element_type=F32)
        state = st_ref[h]
        cross = jnp.dot(q, state.astype(BF16), preferred_element_type=F32) * xi_ref[h]
        o = intra + cross
        kz = (k.astype(F32) * zeta_ref[h]).astype(BF16)
        st_ref[h] = state * gamma_c[h] + lax.dot_general(
            kz, v, (((0,), (0,)), ((), ())), preferred_element_type=F32)
        mu = jnp.mean(o, axis=-1, keepdims=True)
        d = o - mu
        var = jnp.mean(d * d, axis=-1, keepdims=True)
        on = d * lax.rsqrt(var + EPS)
        gate = g_ref[:, h * V_DIM:(h + 1) * V_DIM].astype(F32)
        gated_ref[:, h * V_DIM:(h + 1) * V_DIM] = (_silu(gate) * on).astype(BF16)

    o_ref[...] = h_ref[...] + jnp.dot(gated_ref[...], wo_ref[...], preferred_element_type=F32)


def _retention_consts(C):
    log_gamma = np.log1p(-(2.0 ** (-5.0 - np.arange(HEADS, dtype=np.float64))))
    idx = np.arange(C, dtype=np.float64)
    rel = idx[:, None] - idx[None, :]
    decay = np.where(rel[None] >= 0, np.exp(np.maximum(rel, 0.0)[None] * log_gamma[:, None, None]), 0.0)
    xi = np.exp((idx + 1.0)[None, :] * log_gamma[:, None])
    zeta = np.exp((C - 1.0 - idx)[None, :] * log_gamma[:, None])
    gamma_c = tuple(float(x) for x in np.exp(C * log_gamma))
    xi_b = np.broadcast_to(xi[:, :, None], (HEADS, C, V_DIM))
    zeta_b = np.broadcast_to(zeta[:, :, None], (HEADS, C, QK_DIM))
    return (gamma_c, jnp.asarray(decay, F32), jnp.asarray(xi_b, F32), jnp.asarray(zeta_b, F32))


def _retention(proj, w_out, h, B, S):
    T, D = h.shape
    C = RET_CHUNK
    N = S // C
    qk_w = HEADS * QK_DIM
    v_w = HEADS * V_DIM
    gamma_c, decay, xi_b, zeta_b = _retention_consts(C)
    row = lambda b, n: b * N + n
    return pl.pallas_call(
        functools.partial(_retention_kernel, gamma_c),
        grid=(B, N),
        in_specs=[
            pl.BlockSpec((C, qk_w), lambda b, n: (row(b, n), 0)),
            pl.BlockSpec((C, qk_w), lambda b, n: (row(b, n), 1)),
            pl.BlockSpec((C, v_w), lambda b, n: (row(b, n), 1)),
            pl.BlockSpec((C, v_w), lambda b, n: (row(b, n), 2)),
            _resident((HEADS, C, C)),
            _resident((HEADS, C, V_DIM)),
            _resident((HEADS, C, QK_DIM)),
            _resident((v_w, D)),
            pl.BlockSpec((C, D), lambda b, n: (row(b, n), 0)),
        ],
        out_specs=pl.BlockSpec((C, D), lambda b, n: (row(b, n), 0)),
        out_shape=jax.ShapeDtypeStruct((T, D), F32),
        scratch_shapes=[pltpu.VMEM((HEADS, QK_DIM, V_DIM), F32), pltpu.VMEM((C, v_w), BF16)],
        compiler_params=_cparams(("parallel", "arbitrary")),
        name="retention",
    )(proj, proj, proj, proj, decay, xi_b, zeta_b, w_out, h)


def _ffn_kernel(h_ref, g_ref, wgu_ref, wd_ref, o_ref):
    h = h_ref[...]
    hn = _rms(h, g_ref[...]).astype(BF16)
    F = wd_ref.shape[0]
    acc = jnp.zeros(h.shape, F32)
    for c0 in range(0, F, FF_CHUNK):
        gate = jnp.dot(hn, wgu_ref[:, c0:c0 + FF_CHUNK], preferred_element_type=F32)
        up = jnp.dot(hn, wgu_ref[:, F + c0:F + c0 + FF_CHUNK], preferred_element_type=F32)
        act = (_silu(gate) * up).astype(BF16)
        acc = acc + jnp.dot(act, wd_ref[c0:c0 + FF_CHUNK, :], preferred_element_type=F32)
    o_ref[...] = h + acc


def _ffn(h, g, w_gu, w_down, tm=512):
    T, D = h.shape
    F = w_down.shape[0]
    return pl.pallas_call(
        _ffn_kernel,
        grid=(T // tm,),
        in_specs=[
            pl.BlockSpec((tm, D), lambda i: (i, 0)),
            _resident((1, D)),
            _resident((D, 2 * F)),
            _resident((F, D)),
        ],
        out_specs=pl.BlockSpec((tm, D), lambda i: (i, 0)),
        out_shape=jax.ShapeDtypeStruct((T, D), F32),
        compiler_params=_cparams(("parallel",)),
        name="dense_swiglu",
    )(h, g, w_gu, w_down)


def _conv_kernel(h_ref, g_ref, w1_ref, b1_ref, wdw_ref, bdw_ref, lng_ref, lnb_ref, w2_ref, b2_ref,
                 o_ref, u_ref, us_ref, c_ref):
    s = pl.program_id(1)
    ts, D = h_ref.shape

    @pl.when(s == 0)
    def _():
        u_ref[0:HALO, :] = jnp.zeros((HALO, D), F32)

    @pl.when(s > 0)
    def _():
        u_ref[0:HALO, :] = u_ref[ts:ts + HALO, :]

    h = h_ref[...]
    hn = _rms(h, g_ref[...]).astype(BF16)
    a = jnp.dot(hn, w1_ref[...], preferred_element_type=F32) + b1_ref[...]
    u_ref[HALO:HALO + ts, :] = a[:, :D] * jax.nn.sigmoid(a[:, D:])

    L = ts + HALO - SUBLANES
    for k in range(1, SUBLANES):
        us_ref[k - 1, 0:L, :] = u_ref[k:k + L, :]

    rc = 32
    base = HALO - (CONV_WIDTH - 1)

    cw = D // 2

    def chunk(r, carry):
        r0 = pl.multiple_of(r * rc, rc)
        for c0 in range(0, D, cw):
            acc = jnp.zeros((rc, cw), F32)
            for j in range(CONV_WIDTH):
                al, ph = divmod(base + j, SUBLANES)
                rows = pl.ds(r0 + al * SUBLANES, rc)
                tap = u_ref[rows, c0:c0 + cw] if ph == 0 else us_ref[ph - 1, rows, c0:c0 + cw]
                acc = acc + tap * wdw_ref[j:j + 1, c0:c0 + cw]
            c_ref[pl.ds(r0, rc), c0:c0 + cw] = acc
        return carry

    lax.fori_loop(0, ts // rc, chunk, 0)

    c = c_ref[...] + bdw_ref[...]
    mu = jnp.mean(c, axis=-1, keepdims=True)
    d = c - mu
    var = jnp.mean(d * d, axis=-1, keepdims=True)
    un = d * lax.rsqrt(var + EPS) * lng_ref[...] + lnb_ref[...]
    act = _silu(un).astype(BF16)
    o_ref[...] = h + jnp.dot(act, w2_ref[...], preferred_element_type=F32) + b2_ref[...]


def _conv_module(h, g, w1, b1, wdw, bdw, lng, lnb, w2, b2, B, S, ts=512):
    T, D = h.shape
    ns = S // ts
    W = wdw.shape[0]
    row = lambda b, s: (b * ns + s, 0)
    return pl.pallas_call(
        _conv_kernel,
        grid=(B, ns),
        in_specs=[
            pl.BlockSpec((ts, D), row),
            _resident((1, D)),
            _resident((D, 2 * D)),
            _resident((1, 2 * D)),
            _resident((W, D)),
            _resident((1, D)),
            _resident((1, D)),
            _resident((1, D)),
            _resident((D, D)),
            _resident((1, D)),
        ],
        out_specs=pl.BlockSpec((ts, D), row),
        out_shape=jax.ShapeDtypeStruct((T, D), F32),
        scratch_shapes=[pltpu.VMEM((ts + HALO, D), F32), pltpu.VMEM((SUBLANES - 1, ts + HALO, D), F32),
                        pltpu.VMEM((ts, D), F32)],
        compiler_params=_cparams(("parallel", "arbitrary")),
        name="conv_module",
    )(h, g, w1, b1, wdw, bdw, lng, lnb, w2, b2)


def _router_kernel(h_ref, g_ref, wr_ref, hn_ref, meta_ref, cnt_ref, run_ref):
    i = pl.program_id(0)
    tm = h_ref.shape[0]

    @pl.when(i == 0)
    def _():
        run_ref[...] = jnp.zeros_like(run_ref)

    hn = _rms(h_ref[...], g_ref[...])
    hn_ref[...] = hn
    hi = hn.astype(BF16)
    lo = (hn - hi.astype(F32)).astype(BF16)
    w = wr_ref[...]
    hw = jnp.dot(hi, w, preferred_element_type=F32)
    logits = hw[:, :LANES] + (hw[:, LANES:] + jnp.dot(lo, w[:, :LANES], preferred_element_type=F32))
    lane = lax.broadcasted_iota(jnp.int32, logits.shape, 1)
    neg = jnp.float32(-jnp.inf)
    logits = jnp.where(lane < N_EXPERTS, logits, neg)
    m1 = jnp.max(logits, axis=-1, keepdims=True)
    i1 = jnp.min(jnp.where(logits == m1, lane, LANES), axis=-1, keepdims=True)
    rest = jnp.where(lane == i1, neg, logits)
    m2 = jnp.max(rest, axis=-1, keepdims=True)
    i2 = jnp.min(jnp.where(rest == m2, lane, LANES), axis=-1, keepdims=True)
    e = jnp.exp(m2 - m1)
    g1 = 1.0 / (1.0 + e)
    g2 = e / (1.0 + e)

    oh1 = (lane == i1).astype(F32)
    oh2 = (lane == i2).astype(F32)
    sel = oh1 + oh2
    r_i = lax.broadcasted_iota(jnp.int32, (tm, tm), 0)
    c_i = lax.broadcasted_iota(jnp.int32, (tm, tm), 1)
    tri = (c_i < r_i).astype(BF16)
    prefix = jnp.dot(tri, sel.astype(BF16), preferred_element_type=F32) + run_ref[...]
    rank1 = jnp.sum(oh1 * prefix, axis=-1, keepdims=True)
    rank2 = jnp.sum(oh2 * prefix, axis=-1, keepdims=True)
    run_ref[...] = run_ref[...] + jnp.sum(sel, axis=0, keepdims=True)
    cnt_ref[...] = jnp.broadcast_to(run_ref[...], cnt_ref.shape)

    meta = jnp.where(lane == 0, i1.astype(F32), 0.0)
    meta = jnp.where(lane == 1, i2.astype(F32), meta)
    meta = jnp.where(lane == 2, rank1, meta)
    meta = jnp.where(lane == 3, rank2, meta)
    meta = jnp.where(lane == 4, g1, meta)
    meta = jnp.where(lane == 5, g2, meta)
    meta_ref[...] = meta


def _router(h, g, wr_pad, tm=512):
    T, D = h.shape
    return pl.pallas_call(
        _router_kernel,
        grid=(T // tm,),
        in_specs=[
            pl.BlockSpec((tm, D), lambda i: (i, 0)),
            _resident((1, D)),
            _resident((D, 2 * LANES)),
        ],
        out_specs=[
            pl.BlockSpec((tm, D), lambda i: (i, 0)),
            pl.BlockSpec((tm, LANES), lambda i: (i, 0)),
            pl.BlockSpec((8, LANES), lambda i: (0, 0)),
        ],
        out_shape=[
            jax.ShapeDtypeStruct((T, D), F32),
            jax.ShapeDtypeStruct((T, LANES), F32),
            jax.ShapeDtypeStruct((8, LANES), F32),
        ],
        scratch_shapes=[pltpu.VMEM((1, LANES), F32)],
        compiler_params=_cparams(("arbitrary",)),
        name="moe_router",
    )(h, g, wr_pad)


def _moe_ffn_kernel(te_ref, nv_ref, gsrc_ref, sdst_ref, hn_hbm, wg_ref, wu_ref, wd_ref, y_hbm,
                    xbuf, xb, acc, ystage, gsem, ssem):
    del te_ref
    i = pl.program_id(0)
    j = pl.program_id(1)
    ni = pl.num_programs(0)
    nf = pl.num_programs(1)
    tm = xb.shape[0]
    rpj = tm // nf
    gslot = lax.rem(i, 2)
    cslot = 1 - gslot
    sslot = gslot
    nv = nv_ref[i]

    def gather_all(slot):
        return pltpu.make_async_copy(hn_hbm.at[pl.ds(0, tm), :], xbuf.at[slot], gsem.at[slot])

    def scatter_wait(slot, n):
        n8 = pl.multiple_of(lax.shift_left(lax.shift_right_logical(n, 3), 3), SUBLANES)

        @pl.when(n8 > 0)
        def _():
            pltpu.make_async_copy(ystage.at[slot, pl.ds(0, n8), :], y_hbm.at[pl.ds(0, n8), :],
                                  ssem.at[slot]).wait()

        def one(k, carry):
            pltpu.make_async_copy(ystage.at[slot, pl.ds(0, 1), :], y_hbm.at[pl.ds(0, 1), :],
                                  ssem.at[slot]).wait()
            return carry

        lax.fori_loop(0, n - n8, one, 0)

    @pl.when(j == 0)
    def _():
        @pl.when(i > 0)
        def _():
            gather_all(cslot).wait()
            xb[...] = xbuf[cslot].astype(BF16)

        acc[...] = jnp.zeros_like(acc)

    for u in range(rpj):
        r = j * rpj + u
        pltpu.make_async_copy(hn_hbm.at[pl.ds(gsrc_ref[r], 1), :], xbuf.at[gslot, pl.ds(r, 1), :],
                              gsem.at[gslot]).start(priority=1)

        @pl.when(r < nv)
        def _():
            pltpu.make_async_copy(ystage.at[sslot, pl.ds(r, 1), :], y_hbm.at[pl.ds(sdst_ref[r], 1), :],
                                  ssem.at[sslot]).start(priority=1)

    @pl.when(nv_ref[i + 1] > 0)
    def _():
        x = xb[...]
        gate = jnp.dot(x, wg_ref[0], preferred_element_type=F32)
        up = jnp.dot(x, wu_ref[0], preferred_element_type=F32)
        act = (_silu(gate) * up).astype(BF16)
        acc[...] += jnp.dot(act, wd_ref[0], preferred_element_type=F32)

    @pl.when(j == nf - 1)
    def _():
        @pl.when(i > 0)
        def _():
            scatter_wait(cslot, nv_ref[i - 1])

        ystage[cslot] = acc[...]

        @pl.when(i == ni - 1)
        def _():
            gather_all(gslot).wait()
            scatter_wait(sslot, nv)


def _moe_ffn(te_steps, nv_steps, gsrc, sdst, hn, w_gu, w_down, tm, tf=FF_CHUNK):
    T, D = hn.shape
    F = w_down.shape[1]
    nf = F // tf
    n_steps = gsrc.shape[0] // SMEM_BLOCK
    grid_spec = pltpu.PrefetchScalarGridSpec(
        num_scalar_prefetch=2,
        grid=(n_steps, nf),
        in_specs=[
            pl.BlockSpec((SMEM_BLOCK,), lambda i, j, te, nv: (i,), memory_space=pltpu.SMEM),
            pl.BlockSpec((SMEM_BLOCK,), lambda i, j, te, nv: (i,), memory_space=pltpu.SMEM),
            pl.BlockSpec(memory_space=pl.ANY),
            pl.BlockSpec((1, D, tf), lambda i, j, te, nv: (te[i], 0, j)),
            pl.BlockSpec((1, D, tf), lambda i, j, te, nv: (te[i], 0, j + nf)),
            pl.BlockSpec((1, tf, D), lambda i, j, te, nv: (te[i], j, 0)),
        ],
        out_specs=pl.BlockSpec(memory_space=pl.ANY),
        scratch_shapes=[
            pltpu.VMEM((2, tm, D), F32),
            pltpu.VMEM((tm, D), BF16),
            pltpu.VMEM((tm, D), F32),
            pltpu.VMEM((2, tm, D), F32),
            pltpu.SemaphoreType.DMA((2,)),
            pltpu.SemaphoreType.DMA((2,)),
        ],
    )
    return pl.pallas_call(
        _moe_ffn_kernel,
        grid_spec=grid_spec,
        out_shape=jax.ShapeDtypeStruct((2 * T, D), F32),
        compiler_params=_cparams(("arbitrary", "arbitrary")),
        name="moe_swiglu",
    )(te_steps, nv_steps, gsrc, sdst, hn, w_gu, w_gu, w_down)


def _combine_kernel(h_ref, meta_ref, g_ref, y1_ref, y2_ref, o_ref):
    meta = meta_ref[...]
    h = h_ref[...] + meta[:, 4:5] * y1_ref[...] + meta[:, 5:6] * y2_ref[...]
    o_ref[...] = _rms(h, g_ref[...])


def _combine(h, meta, g, y, tb=512):
    T, D = h.shape
    nb = T // tb
    return pl.pallas_call(
        _combine_kernel,
        grid=(nb,),
        in_specs=[
            pl.BlockSpec((tb, D), lambda i: (i, 0)),
            pl.BlockSpec((tb, LANES), lambda i: (i, 0)),
            _resident((1, D)),
            pl.BlockSpec((tb, D), lambda i: (i, 0)),
            pl.BlockSpec((tb, D), lambda i: (i + nb, 0)),
        ],
        out_specs=pl.BlockSpec((tb, D), lambda i: (i, 0)),
        out_shape=jax.ShapeDtypeStruct((T, D), F32),
        compiler_params=_cparams(("parallel",)),
        name="moe_combine",
    )(h, meta, g, y, y)


def _moe_layer(h, g_ffn, g_final, w_router, w_gu, w_down, tm=MOE_TILE):
    T, D = h.shape
    E = w_router.shape[1]
    wr_pad = jnp.zeros((D, LANES), F32).at[:, :E].set(w_router)
    wr_hi = wr_pad.astype(BF16)
    wr_lo = (wr_pad - wr_hi.astype(F32)).astype(BF16)
    hn, meta, cnt = _router(h, g_ffn, jnp.concatenate([wr_hi, wr_lo], axis=1))

    counts = cnt[0, :E].astype(jnp.int32)
    padded = ((counts + tm - 1) // tm) * tm
    ends = jnp.cumsum(padded)
    offsets = ends - padded
    e1 = meta[:, 0].astype(jnp.int32)
    e2 = meta[:, 1].astype(jnp.int32)
    pos1 = offsets[e1] + meta[:, 2].astype(jnp.int32)
    pos2 = offsets[e2] + meta[:, 3].astype(jnp.int32)
    n_tiles = -(-(2 * T + E * (tm - 1)) // tm)
    R = n_tiles * tm
    n_steps = n_tiles + 2
    tok = jnp.arange(T, dtype=jnp.int32)
    pos = jnp.concatenate([pos1, pos2])
    dst = jnp.zeros((R,), jnp.int32).at[pos].set(jnp.concatenate([tok, T + tok]))
    src = jnp.where(dst >= T, dst - T, dst)
    gsrc = jnp.concatenate([src, jnp.zeros((2 * tm,), jnp.int32)])
    sdst = jnp.concatenate([jnp.zeros((2 * tm,), jnp.int32), dst])
    step_block = lambda t: jnp.pad(t.reshape(n_steps, tm), ((0, 0), (0, SMEM_BLOCK - tm))).reshape(-1)
    gsrc, sdst = step_block(gsrc), step_block(sdst)
    tile_start = jnp.arange(n_tiles, dtype=jnp.int32) * tm
    tile_expert = jnp.minimum(jnp.searchsorted(ends, tile_start, side="right"), E - 1).astype(jnp.int32)
    te_steps = tile_expert[jnp.clip(jnp.arange(n_steps) - 1, 0, n_tiles - 1)]
    tile_rows = jnp.clip((offsets + counts)[tile_expert] - tile_start, 0, tm).astype(jnp.int32)
    nv_steps = jnp.concatenate([jnp.zeros((2,), jnp.int32), tile_rows, jnp.zeros((1,), jnp.int32)])

    y = _moe_ffn(te_steps, nv_steps, gsrc, sdst, hn, w_gu, w_down, tm)
    return _combine(h, meta, g_final, y)


def kernel(x, positions, norm_mix_g, norm_ffn_g, final_norm_g, ret_w_in, ret_w_out, conv_w_pw1, conv_b_pw1, conv_w_dw, conv_b_dw, conv_ln_g, conv_ln_b, conv_w_pw2, conv_b_pw2, ffn_w_gu, ffn_w_down, moe_w_router, moe_w_gu, moe_w_down):
    B, S, D = x.shape
    T = B * S
    h = x.reshape(T, D)
    row = lambda v: v.reshape(1, -1)

    posb = jnp.broadcast_to(positions.reshape(T, 1), (T, LANES))
    half = QK_DIM // 2
    inv = (ROPE_BASE ** (-jnp.arange(half, dtype=F32) / half)).reshape(1, half)
    proj = _inproj(h, row(norm_mix_g[0]), posb, inv, ret_w_in[0].astype(BF16))
    h = _retention(proj, ret_w_out[0].astype(BF16), h, B, S)
    h = _ffn(h, row(norm_ffn_g[0]), ffn_w_gu[0].astype(BF16), ffn_w_down[0].astype(BF16))
    h = _conv_module(h, row(norm_mix_g[1]), conv_w_pw1[0].astype(BF16), row(conv_b_pw1[0]),
                     conv_w_dw[0], row(conv_b_dw[0]), row(conv_ln_g[0]), row(conv_ln_b[0]),
                     conv_w_pw2[0].astype(BF16), row(conv_b_pw2[0]), B, S)
    out = _moe_layer(h, row(norm_ffn_g[1]), row(final_norm_g), moe_w_router[0],
                     moe_w_gu[0].astype(BF16), moe_w_down[0].astype(BF16))
    return out.reshape(B, S, D)
```

```python
import functools

import numpy as np
import jax
import jax.numpy as jnp
from jax import lax
from jax.experimental import pallas as pl
from jax.experimental.pallas import tpu as pltpu
from jax.experimental.pallas import tpu_sc as plsc

F32 = jnp.float32
BF16 = jnp.bfloat16

EPS = 1e-6
ROPE_BASE = 10000.0
HEADS = 4
QK_DIM = 256
V_DIM = 512
CONV_WIDTH = 31
N_EXPERTS = 8
LANES = 128
SUBLANES = 8
HALO = 32
VMEM_LIMIT = 56 * 1024 * 1024

RET_CHUNK = 256
FF_CHUNK = 512
MOE_TILE = 1024
SC_CHUNK = 32


def _cparams(sem):
    return pltpu.CompilerParams(dimension_semantics=sem, vmem_limit_bytes=VMEM_LIMIT)


def _resident(shape):
    zeros = (0,) * len(shape)
    return pl.BlockSpec(shape, lambda *_: zeros, pipeline_mode=pl.Buffered(1))


def _rms(h, g):
    return h * lax.rsqrt(jnp.mean(h * h, axis=-1, keepdims=True) + EPS) * g


def _silu(x):
    return x * jax.nn.sigmoid(x)


def _inproj_kernel(h_ref, g_ref, pos_ref, inv_ref, w_ref, o_ref):
    hn = _rms(h_ref[...], g_ref[...]).astype(BF16)
    ang = pos_ref[...].astype(F32) * inv_ref[...]
    c = jnp.cos(ang)
    s = jnp.sin(ang)
    half = QK_DIM // 2
    qk_w = HEADS * QK_DIM
    for hd in range(2 * HEADS):
        col = hd * QK_DIM
        a = jnp.dot(hn, w_ref[:, col:col + QK_DIM], preferred_element_type=F32)
        t1 = a[:, :half]
        t2 = a[:, half:]
        scale = 1.0 if hd < HEADS else QK_DIM ** -0.5
        o_ref[:, col:col + half] = ((t1 * c - t2 * s) * scale).astype(BF16)
        o_ref[:, col + half:col + QK_DIM] = ((t2 * c + t1 * s) * scale).astype(BF16)
    blk = 1024
    for col in range(2 * qk_w, w_ref.shape[1], blk):
        o_ref[:, col:col + blk] = jnp.dot(hn, w_ref[:, col:col + blk], preferred_element_type=F32).astype(BF16)


def _inproj(h, g, posb, inv, w, tm=512):
    T, D = h.shape
    N = w.shape[1]
    return pl.pallas_call(
        _inproj_kernel,
        grid=(T // tm,),
        in_specs=[
            pl.BlockSpec((tm, D), lambda i: (i, 0)),
            _resident((1, D)),
            pl.BlockSpec((tm, LANES), lambda i: (i, 0)),
            _resident((1, LANES)),
            _resident((D, N)),
        ],
        out_specs=pl.BlockSpec((tm, N), lambda i: (i, 0)),
        out_shape=jax.ShapeDtypeStruct((T, N), BF16),
        compiler_params=_cparams(("parallel",)),
        name="ret_inproj",
    )(h, g, posb, inv, w)


def _retention_kernel(gamma_c, q_ref, k_ref, v_ref, g_ref, dec_ref, xi_ref, zeta_ref, wo_ref, h_ref,
                      o_ref, st_ref, gated_ref):
    n = pl.program_id(1)

    @pl.when(n == 0)
    def _():
        st_ref[...] = jnp.zeros_like(st_ref)

    for h in range(HEADS):
        q = q_ref[:, h * QK_DIM:(h + 1) * QK_DIM]
        k = k_ref[:, h * QK_DIM:(h + 1) * QK_DIM]
        v = v_ref[:, h * V_DIM:(h + 1) * V_DIM]
        s = lax.dot_general(q, k, (((1,), (1,)), ((), ())), preferred_element_type=F32)
        s = s * dec_ref[h]
        intra = jnp.dot(s.astype(BF16), v, preferred_element_type=F32)
        state = st_ref[h]
        cross = jnp.dot(q, state.astype(BF16), preferred_element_type=F32) * xi_ref[h]
        o = intra + cross
        kz = (k.astype(F32) * zeta_ref[h]).astype(BF16)
        st_ref[h] = state * gamma_c[h] + lax.dot_general(
            kz, v, (((0,), (0,)), ((), ())), preferred_element_type=F32)
        mu = jnp.mean(o, axis=-1, keepdims=True)
        d = o - mu
        var = jnp.mean(d * d, axis=-1, keepdims=True)
        on = d * lax.rsqrt(var + EPS)
        gate = g_ref[:, h * V_DIM:(h + 1) * V_DIM].astype(F32)
        gated_ref[:, h * V_DIM:(h + 1) * V_DIM] = (_silu(gate) * on).astype(BF16)

    o_ref[...] = h_ref[...] + jnp.dot(gated_ref[...], wo_ref[...], preferred_element_type=F32)


def _retention_consts(C):
    log_gamma = np.log1p(-(2.0 ** (-5.0 - np.arange(HEADS, dtype=np.float64))))
    idx = np.arange(C, dtype=np.float64)
    rel = idx[:, None] - idx[None, :]
    decay = np.where(rel[None] >= 0, np.exp(np.maximum(rel, 0.0)[None] * log_gamma[:, None, None]), 0.0)
    xi = np.exp((idx + 1.0)[None, :] * log_gamma[:, None])
    zeta = np.exp((C - 1.0 - idx)[None, :] * log_gamma[:, None])
    gamma_c = tuple(float(x) for x in np.exp(C * log_gamma))
    xi_b = np.broadcast_to(xi[:, :, None], (HEADS, C, V_DIM))
    zeta_b = np.broadcast_to(zeta[:, :, None], (HEADS, C, QK_DIM))
    return (gamma_c, jnp.asarray(decay, F32), jnp.asarray(xi_b, F32), jnp.asarray(zeta_b, F32))


def _retention(proj, w_out, h, B, S):
    T, D = h.shape
    C = RET_CHUNK
    N = S // C
    qk_w = HEADS * QK_DIM
    v_w = HEADS * V_DIM
    gamma_c, decay, xi_b, zeta_b = _retention_consts(C)
    row = lambda b, n: b * N + n
    return pl.pallas_call(
        functools.partial(_retention_kernel, gamma_c),
        grid=(B, N),
        in_specs=[
            pl.BlockSpec((C, qk_w), lambda b, n: (row(b, n), 0)),
            pl.BlockSpec((C, qk_w), lambda b, n: (row(b, n), 1)),
            pl.BlockSpec((C, v_w), lambda b, n: (row(b, n), 1)),
            pl.BlockSpec((C, v_w), lambda b, n: (row(b, n), 2)),
            _resident((HEADS, C, C)),
            _resident((HEADS, C, V_DIM)),
            _resident((HEADS, C, QK_DIM)),
            _resident((v_w, D)),
            pl.BlockSpec((C, D), lambda b, n: (row(b, n), 0)),
        ],
        out_specs=pl.BlockSpec((C, D), lambda b, n: (row(b, n), 0)),
        out_shape=jax.ShapeDtypeStruct((T, D), F32),
        scratch_shapes=[pltpu.VMEM((HEADS, QK_DIM, V_DIM), F32), pltpu.VMEM((C, v_w), BF16)],
        compiler_params=_cparams(("parallel", "arbitrary")),
        name="retention",
    )(proj, proj, proj, proj, decay, xi_b, zeta_b, w_out, h)


def _ffn_kernel(h_ref, g_ref, wgu_ref, wd_ref, o_ref):
    h = h_ref[...]
    hn = _rms(h, g_ref[...]).astype(BF16)
    F = wd_ref.shape[0]
    acc = jnp.zeros(h.shape, F32)
    for c0 in range(0, F, FF_CHUNK):
        gate = jnp.dot(hn, wgu_ref[:, c0:c0 + FF_CHUNK], preferred_element_type=F32)
        up = jnp.dot(hn, wgu_ref[:, F + c0:F + c0 + FF_CHUNK], preferred_element_type=F32)
        act = (_silu(gate) * up).astype(BF16)
        acc = acc + jnp.dot(act, wd_ref[c0:c0 + FF_CHUNK, :], preferred_element_type=F32)
    o_ref[...] = h + acc


def _ffn(h, g, w_gu, w_down, tm=512):
    T, D = h.shape
    F = w_down.shape[0]
    return pl.pallas_call(
        _ffn_kernel,
        grid=(T // tm,),
        in_specs=[
            pl.BlockSpec((tm, D), lambda i: (i, 0)),
            _resident((1, D)),
            _resident((D, 2 * F)),
            _resident((F, D)),
        ],
        out_specs=pl.BlockSpec((tm, D), lambda i: (i, 0)),
        out_shape=jax.ShapeDtypeStruct((T, D), F32),
        compiler_params=_cparams(("parallel",)),
        name="dense_swiglu",
    )(h, g, w_gu, w_down)


def _conv_kernel(h_ref, g_ref, w1_ref, b1_ref, wdw_ref, bdw_ref, lng_ref, lnb_ref, w2_ref, b2_ref,
                 o_ref, u_ref, us_ref, c_ref):
    s = pl.program_id(1)
    ts, D = h_ref.shape

    @pl.when(s == 0)
    def _():
        u_ref[0:HALO, :] = jnp.zeros((HALO, D), F32)

    @pl.when(s > 0)
    def _():
        u_ref[0:HALO, :] = u_ref[ts:ts + HALO, :]

    h = h_ref[...]
    hn = _rms(h, g_ref[...]).astype(BF16)
    a = jnp.dot(hn, w1_ref[...], preferred_element_type=F32) + b1_ref[...]
    u_ref[HALO:HALO + ts, :] = a[:, :D] * jax.nn.sigmoid(a[:, D:])

    L = ts + HALO - SUBLANES
    for k in range(1, SUBLANES):
        us_ref[k - 1, 0:L, :] = u_ref[k:k + L, :]

    rc = 32
    base = HALO - (CONV_WIDTH - 1)

    cw = D // 2

    def chunk(r, carry):
        r0 = pl.multiple_of(r * rc, rc)
        for c0 in range(0, D, cw):
            acc = jnp.zeros((rc, cw), F32)
            for j in range(CONV_WIDTH):
                al, ph = divmod(base + j, SUBLANES)
                rows = pl.ds(r0 + al * SUBLANES, rc)
                tap = u_ref[rows, c0:c0 + cw] if ph == 0 else us_ref[ph - 1, rows, c0:c0 + cw]
                acc = acc + tap * wdw_ref[j:j + 1, c0:c0 + cw]
            c_ref[pl.ds(r0, rc), c0:c0 + cw] = acc
        return carry

    lax.fori_loop(0, ts // rc, chunk, 0)

    c = c_ref[...] + bdw_ref[...]
    mu = jnp.mean(c, axis=-1, keepdims=True)
    d = c - mu
    var = jnp.mean(d * d, axis=-1, keepdims=True)
    un = d * lax.rsqrt(var + EPS) * lng_ref[...] + lnb_ref[...]
    act = _silu(un).astype(BF16)
    o_ref[...] = h + jnp.dot(act, w2_ref[...], preferred_element_type=F32) + b2_ref[...]


def _conv_module(h, g, w1, b1, wdw, bdw, lng, lnb, w2, b2, B, S, ts=512):
    T, D = h.shape
    ns = S // ts
    W = wdw.shape[0]
    row = lambda b, s: (b * ns + s, 0)
    return pl.pallas_call(
        _conv_kernel,
        grid=(B, ns),
        in_specs=[
            pl.BlockSpec((ts, D), row),
            _resident((1, D)),
            _resident((D, 2 * D)),
            _resident((1, 2 * D)),
            _resident((W, D)),
            _resident((1, D)),
            _resident((1, D)),
            _resident((1, D)),
            _resident((D, D)),
            _resident((1, D)),
        ],
        out_specs=pl.BlockSpec((ts, D), row),
        out_shape=jax.ShapeDtypeStruct((T, D), F32),
        scratch_shapes=[pltpu.VMEM((ts + HALO, D), F32), pltpu.VMEM((SUBLANES - 1, ts + HALO, D), F32),
                        pltpu.VMEM((ts, D), F32)],
        compiler_params=_cparams(("parallel", "arbitrary")),
        name="conv_module",
    )(h, g, w1, b1, wdw, bdw, lng, lnb, w2, b2)


def _router_kernel(h_ref, g_ref, wr_ref, hn_ref, meta_ref, cnt_ref, run_ref):
    i = pl.program_id(0)
    tm = h_ref.shape[0]

    @pl.when(i == 0)
    def _():
        run_ref[...] = jnp.zeros_like(run_ref)

    hn = _rms(h_ref[...], g_ref[...])
    hn_ref[...] = hn
    hi = hn.astype(BF16)
    lo = (hn - hi.astype(F32)).astype(BF16)
    w = wr_ref[...]
    hw = jnp.dot(hi, w, preferred_element_type=F32)
    logits = hw[:, :LANES] + (hw[:, LANES:] + jnp.dot(lo, w[:, :LANES], preferred_element_type=F32))
    lane = lax.broadcasted_iota(jnp.int32, logits.shape, 1)
    neg = jnp.float32(-jnp.inf)
    logits = jnp.where(lane < N_EXPERTS, logits, neg)
    m1 = jnp.max(logits, axis=-1, keepdims=True)
    i1 = jnp.min(jnp.where(logits == m1, lane, LANES), axis=-1, keepdims=True)
    rest = jnp.where(lane == i1, neg, logits)
    m2 = jnp.max(rest, axis=-1, keepdims=True)
    i2 = jnp.min(jnp.where(rest == m2, lane, LANES), axis=-1, keepdims=True)
    e = jnp.exp(m2 - m1)
    g1 = 1.0 / (1.0 + e)
    g2 = e / (1.0 + e)

    oh1 = (lane == i1).astype(F32)
    oh2 = (lane == i2).astype(F32)
    sel = oh1 + oh2
    r_i = lax.broadcasted_iota(jnp.int32, (tm, tm), 0)
    c_i = lax.broadcasted_iota(jnp.int32, (tm, tm), 1)
    tri = (c_i < r_i).astype(BF16)
    prefix = jnp.dot(tri, sel.astype(BF16), preferred_element_type=F32) + run_ref[...]
    rank1 = jnp.sum(oh1 * prefix, axis=-1, keepdims=True)
    rank2 = jnp.sum(oh2 * prefix, axis=-1, keepdims=True)
    run_ref[...] = run_ref[...] + jnp.sum(sel, axis=0, keepdims=True)
    cnt_ref[...] = jnp.broadcast_to(run_ref[...], cnt_ref.shape)

    meta = jnp.where(lane == 0, i1.astype(F32), 0.0)
    meta = jnp.where(lane == 1, i2.astype(F32), meta)
    meta = jnp.where(lane == 2, rank1, meta)
    meta = jnp.where(lane == 3, rank2, meta)
    meta = jnp.where(lane == 4, g1, meta)
    meta = jnp.where(lane == 5, g2, meta)
    meta_ref[...] = meta


def _router(h, g, wr_pad, tm=512):
    T, D = h.shape
    return pl.pallas_call(
        _router_kernel,
        grid=(T // tm,),
        in_specs=[
            pl.BlockSpec((tm, D), lambda i: (i, 0)),
            _resident((1, D)),
            _resident((D, 2 * LANES)),
        ],
        out_specs=[
            pl.BlockSpec((tm, D), lambda i: (i, 0)),
            pl.BlockSpec((tm, LANES), lambda i: (i, 0)),
            pl.BlockSpec((8, LANES), lambda i: (0, 0)),
        ],
        out_shape=[
            jax.ShapeDtypeStruct((T, D), F32),
            jax.ShapeDtypeStruct((T, LANES), F32),
            jax.ShapeDtypeStruct((8, LANES), F32),
        ],
        scratch_shapes=[pltpu.VMEM((1, LANES), F32)],
        compiler_params=_cparams(("arbitrary",)),
        name="moe_router",
    )(h, g, wr_pad)


def _sc_gather(table, idx):
    V, D = table.shape
    B = idx.shape[0]
    mesh = plsc.VectorSubcoreMesh(core_axis_name="c", subcore_axis_name="s")
    nw = mesh.num_cores * mesh.num_subcores
    k = SC_CHUNK
    b_per_w = B // nw
    n_pairs = b_per_w // (2 * k)
    assert B % (nw * 2 * k) == 0, (B, nw, k)

    @functools.partial(
        pl.kernel, mesh=mesh,
        out_type=jax.ShapeDtypeStruct((B, D), table.dtype),
        scratch_types=[
            pltpu.VMEM((b_per_w,), jnp.int32),
            pltpu.VMEM((k, D), table.dtype),
            pltpu.VMEM((k, D), table.dtype),
            pltpu.SemaphoreType.DMA,
            pltpu.SemaphoreType.DMA,
        ],
    )
    def gather_kernel(table_hbm, idx_hbm, out_hbm, idx_v, buf0, buf1, sem0, sem1):
        wid = lax.axis_index("s") * mesh.num_cores + lax.axis_index("c")
        base = pl.multiple_of(wid * b_per_w, SUBLANES)
        pltpu.sync_copy(idx_hbm.at[pl.ds(base, b_per_w)], idx_v)

        def gather(c, buf, sem):
            rows = idx_v.at[pl.ds(pl.multiple_of(c * k, SUBLANES), k)]
            return pltpu.make_async_copy(table_hbm.at[rows], buf, sem)

        def store(c, buf):
            pltpu.sync_copy(buf, out_hbm.at[pl.ds(pl.multiple_of(base + c * k, SUBLANES), k)])

        gather(0, buf0, sem0).start()

        @pl.loop(0, n_pairs)
        def _(p):
            c = 2 * p
            gather(c + 1, buf1, sem1).start()
            gather(c, buf0, sem0).wait()
            store(c, buf0)

            @pl.when(p + 1 < n_pairs)
            def _():
                gather(c + 2, buf0, sem0).start()

            gather(c + 1, buf1, sem1).wait()
            store(c + 1, buf1)

    return gather_kernel(table, idx)


def _moe_ffn_kernel(te_ref, nv_ref, x_ref, wg_ref, wu_ref, wd_ref, o_ref, xb_ref):
    del te_ref
    i = pl.program_id(0)
    j = pl.program_id(1)
    live = nv_ref[i] > 0

    @pl.when(j == 0)
    def _():
        xb_ref[...] = x_ref[...].astype(BF16)
        o_ref[...] = jnp.zeros_like(o_ref)

    @pl.when(live)
    def _():
        x = xb_ref[...]
        gate = jnp.dot(x, wg_ref[0], preferred_element_type=F32)
        up = jnp.dot(x, wu_ref[0], preferred_element_type=F32)
        act = (_silu(gate) * up).astype(BF16)
        o_ref[...] += jnp.dot(act, wd_ref[0], preferred_element_type=F32)


def _moe_ffn(tile_expert, tile_rows, xs, w_gu, w_down, tm, tf=FF_CHUNK):
    R, D = xs.shape
    F = w_down.shape[1]
    nf = F // tf
    grid_spec = pltpu.PrefetchScalarGridSpec(
        num_scalar_prefetch=2,
        grid=(R // tm, nf),
        in_specs=[
            pl.BlockSpec((tm, D), lambda i, j, te, nv: (i, 0)),
            pl.BlockSpec((1, D, tf), lambda i, j, te, nv: (te[i], 0, j)),
            pl.BlockSpec((1, D, tf), lambda i, j, te, nv: (te[i], 0, j + nf)),
            pl.BlockSpec((1, tf, D), lambda i, j, te, nv: (te[i], j, 0)),
        ],
        out_specs=pl.BlockSpec((tm, D), lambda i, j, te, nv: (i, 0)),
        scratch_shapes=[pltpu.VMEM((tm, D), BF16)],
    )
    return pl.pallas_call(
        _moe_ffn_kernel,
        grid_spec=grid_spec,
        out_shape=jax.ShapeDtypeStruct((R, D), F32),
        compiler_params=_cparams(("arbitrary", "arbitrary")),
        name="moe_swiglu",
    )(tile_expert, tile_rows, xs, w_gu, w_gu, w_down)


def _combine_kernel(h_ref, meta_ref, g_ref, y1_ref, y2_ref, o_ref):
    meta = meta_ref[...]
    h = h_ref[...] + meta[:, 4:5] * y1_ref[...] + meta[:, 5:6] * y2_ref[...]
    o_ref[...] = _rms(h, g_ref[...])


def _combine(h, meta, g, y, tb=512):
    T, D = h.shape
    nb = T // tb
    return pl.pallas_call(
        _combine_kernel,
        grid=(nb,),
        in_specs=[
            pl.BlockSpec((tb, D), lambda i: (i, 0)),
            pl.BlockSpec((tb, LANES), lambda i: (i, 0)),
            _resident((1, D)),
            pl.BlockSpec((tb, D), lambda i: (i, 0)),
            pl.BlockSpec((tb, D), lambda i: (i + nb, 0)),
        ],
        out_specs=pl.BlockSpec((tb, D), lambda i: (i, 0)),
        out_shape=jax.ShapeDtypeStruct((T, D), F32),
        compiler_params=_cparams(("parallel",)),
        name="moe_combine",
    )(h, meta, g, y, y)


def _moe_layer(h, g_ffn, g_final, w_router, w_gu, w_down, tm=MOE_TILE):
    T, D = h.shape
    E = w_router.shape[1]
    wr_pad = jnp.zeros((D, LANES), F32).at[:, :E].set(w_router)
    wr_hi = wr_pad.astype(BF16)
    wr_lo = (wr_pad - wr_hi.astype(F32)).astype(BF16)
    hn, meta, cnt = _router(h, g_ffn, jnp.concatenate([wr_hi, wr_lo], axis=1))

    counts = cnt[0, :E].astype(jnp.int32)
    padded = ((counts + tm - 1) // tm) * tm
    ends = jnp.cumsum(padded)
    offsets = ends - padded
    e1 = meta[:, 0].astype(jnp.int32)
    e2 = meta[:, 1].astype(jnp.int32)
    pos1 = offsets[e1] + meta[:, 2].astype(jnp.int32)
    pos2 = offsets[e2] + meta[:, 3].astype(jnp.int32)
    n_tiles = -(-(2 * T + E * (tm - 1)) // tm)
    R = n_tiles * tm
    tok = jnp.arange(T, dtype=jnp.int32)
    pos = jnp.concatenate([pos1, pos2])
    src = jnp.zeros((R,), jnp.int32).at[pos].set(jnp.concatenate([tok, tok]))
    tile_start = jnp.arange(n_tiles, dtype=jnp.int32) * tm
    tile_expert = jnp.minimum(jnp.searchsorted(ends, tile_start, side="right"), E - 1).astype(jnp.int32)
    tile_rows = jnp.clip((offsets + counts)[tile_expert] - tile_start, 0, tm).astype(jnp.int32)

    xs = _sc_gather(hn, src)
    ys = _moe_ffn(tile_expert, tile_rows, xs, w_gu, w_down, tm)
    y = _sc_gather(ys, pos)
    return _combine(h, meta, g_final, y)


def kernel(x, positions, norm_mix_g, norm_ffn_g, final_norm_g, ret_w_in, ret_w_out, conv_w_pw1, conv_b_pw1, conv_w_dw, conv_b_dw, conv_ln_g, conv_ln_b, conv_w_pw2, conv_b_pw2, ffn_w_gu, ffn_w_down, moe_w_router, moe_w_gu, moe_w_down):
    B, S, D = x.shape
    T = B * S
    h = x.reshape(T, D)
    row = lambda v: v.reshape(1, -1)

    posb = jnp.broadcast_to(positions.reshape(T, 1), (T, LANES))
    half = QK_DIM // 2
    inv = (ROPE_BASE ** (-jnp.arange(half, dtype=F32) / half)).reshape(1, half)
    proj = _inproj(h, row(norm_mix_g[0]), posb, inv, ret_w_in[0].astype(BF16))
    h = _retention(proj, ret_w_out[0].astype(BF16), h, B, S)
    h = _ffn(h, row(norm_ffn_g[0]), ffn_w_gu[0].astype(BF16), ffn_w_down[0].astype(BF16))
    h = _conv_module(h, row(norm_mix_g[1]), conv_w_pw1[0].astype(BF16), row(conv_b_pw1[0]),
                     conv_w_dw[0], row(conv_b_dw[0]), row(conv_ln_g[0]), row(conv_ln_b[0]),
                     conv_w_pw2[0].astype(BF16), row(conv_b_pw2[0]), B, S)
    out = _moe_layer(h, row(norm_ffn_g[1]), row(final_norm_g), moe_w_router[0],
                     moe_w_gu[0].astype(BF16), moe_w_down[0].astype(BF16))
    return out.reshape(B, S, D)
```

```python
import functools

import numpy as np
import jax
import jax.numpy as jnp
from jax import lax
from jax.experimental import pallas as pl
from jax.experimental.pallas import tpu as pltpu
from jax.experimental.pallas import tpu_sc as plsc

F32 = jnp.float32
BF16 = jnp.bfloat16

EPS = 1e-6
ROPE_BASE = 10000.0
HEADS = 4
QK_DIM = 256
V_DIM = 512
CONV_WIDTH = 31
N_EXPERTS = 8
LANES = 128
SUBLANES = 8
HALO = 32
VMEM_LIMIT = 56 * 1024 * 1024

RET_CHUNK = 256
FF_CHUNK = 512
MOE_TILE = 1024
SC_CHUNK_BYTES = 128 * 1024


def _cparams(sem):
    return pltpu.CompilerParams(dimension_semantics=sem, vmem_limit_bytes=VMEM_LIMIT)


def _resident(shape):
    zeros = (0,) * len(shape)
    return pl.BlockSpec(shape, lambda *_: zeros, pipeline_mode=pl.Buffered(1))


def _rms(h, g):
    return h * lax.rsqrt(jnp.mean(h * h, axis=-1, keepdims=True) + EPS) * g


def _silu(x):
    return x * jax.nn.sigmoid(x)


def _pack_pairs(x):
    bits = lax.bitcast_convert_type(x.astype(BF16).astype(F32), jnp.uint32)
    half = x.shape[1] // 2
    words = lax.shift_right_logical(bits[:, :half], jnp.uint32(16)) | (bits[:, half:] & jnp.uint32(0xFFFF0000))
    return lax.bitcast_convert_type(words, jnp.int32)


def _unpack_pairs(words):
    w = lax.bitcast_convert_type(words, jnp.uint32)
    lo = lax.bitcast_convert_type(lax.shift_left(w, jnp.uint32(16)), F32)
    hi = lax.bitcast_convert_type(w & jnp.uint32(0xFFFF0000), F32)
    return lo, hi


def _inproj_kernel(h_ref, g_ref, pos_ref, inv_ref, w_ref, o_ref):
    hn = _rms(h_ref[...], g_ref[...]).astype(BF16)
    ang = pos_ref[...].astype(F32) * inv_ref[...]
    c = jnp.cos(ang)
    s = jnp.sin(ang)
    half = QK_DIM // 2
    qk_w = HEADS * QK_DIM
    for hd in range(2 * HEADS):
        col = hd * QK_DIM
        a = jnp.dot(hn, w_ref[:, col:col + QK_DIM], preferred_element_type=F32)
        t1 = a[:, :half]
        t2 = a[:, half:]
        scale = 1.0 if hd < HEADS else QK_DIM ** -0.5
        o_ref[:, col:col + half] = ((t1 * c - t2 * s) * scale).astype(BF16)
        o_ref[:, col + half:col + QK_DIM] = ((t2 * c + t1 * s) * scale).astype(BF16)
    blk = 1024
    for col in range(2 * qk_w, w_ref.shape[1], blk):
        o_ref[:, col:col + blk] = jnp.dot(hn, w_ref[:, col:col + blk], preferred_element_type=F32).astype(BF16)


def _inproj(h, g, posb, inv, w, tm=512):
    T, D = h.shape
    N = w.shape[1]
    return pl.pallas_call(
        _inproj_kernel,
        grid=(T // tm,),
        in_specs=[
            pl.BlockSpec((tm, D), lambda i: (i, 0)),
            _resident((1, D)),
            pl.BlockSpec((tm, LANES), lambda i: (i, 0)),
            _resident((1, LANES)),
            _resident((D, N)),
        ],
        out_specs=pl.BlockSpec((tm, N), lambda i: (i, 0)),
        out_shape=jax.ShapeDtypeStruct((T, N), BF16),
        compiler_params=_cparams(("parallel",)),
        name="ret_inproj",
    )(h, g, posb, inv, w)


def _retention_kernel(gamma_c, q_ref, k_ref, v_ref, g_ref, dec_ref, xi_ref, zeta_ref, wo_ref, h_ref,
                      o_ref, st_ref, gated_ref):
    n = pl.program_id(1)

    @pl.when(n == 0)
    def _():
        st_ref[...] = jnp.zeros_like(st_ref)

    for h in range(HEADS):
        q = q_ref[:, h * QK_DIM:(h + 1) * QK_DIM]
        k = k_ref[:, h * QK_DIM:(h + 1) * QK_DIM]
        v = v_ref[:, h * V_DIM:(h + 1) * V_DIM]
        s = lax.dot_general(q, k, (((1,), (1,)), ((), ())), preferred_element_type=F32)
        s = s * dec_ref[h]
        intra = jnp.dot(s.astype(BF16), v, preferred_element_type=F32)
        state = st_ref[h]
        cross = jnp.dot(q, state.astype(BF16), preferred_element_type=F32) * xi_ref[h]
        o = intra + cross
        kz = (k.astype(F32) * zeta_ref[h]).astype(BF16)
        st_ref[h] = state * gamma_c[h] + lax.dot_general(
            kz, v, (((0,), (0,)), ((), ())), preferred_element_type=F32)
        mu = jnp.mean(o, axis=-1, keepdims=True)
        d = o - mu
        var = jnp.mean(d * d, axis=-1, keepdims=True)
        on = d * lax.rsqrt(var + EPS)
        gate = g_ref[:, h * V_DIM:(h + 1) * V_DIM].astype(F32)
        gated_ref[:, h * V_DIM:(h + 1) * V_DIM] = (_silu(gate) * on).astype(BF16)

    o_ref[...] = h_ref[...] + jnp.dot(gated_ref[...], wo_ref[...], preferred_element_type=F32)


def _retention_consts(C):
    log_gamma = np.log1p(-(2.0 ** (-5.0 - np.arange(HEADS, dtype=np.float64))))
    idx = np.arange(C, dtype=np.float64)
    rel = idx[:, None] - idx[None, :]
    decay = np.where(rel[None] >= 0, np.exp(np.maximum(rel, 0.0)[None] * log_gamma[:, None, None]), 0.0)
    xi = np.exp((idx + 1.0)[None, :] * log_gamma[:, None])
    zeta = np.exp((C - 1.0 - idx)[None, :] * log_gamma[:, None])
    gamma_c = tuple(float(x) for x in np.exp(C * log_gamma))
    xi_b = np.broadcast_to(xi[:, :, None], (HEADS, C, V_DIM))
    zeta_b = np.broadcast_to(zeta[:, :, None], (HEADS, C, QK_DIM))
    return (gamma_c, jnp.asarray(decay, F32), jnp.asarray(xi_b, F32), jnp.asarray(zeta_b, F32))


def _retention(proj, w_out, h, B, S):
    T, D = h.shape
    C = RET_CHUNK
    N = S // C
    qk_w = HEADS * QK_DIM
    v_w = HEADS * V_DIM
    gamma_c, decay, xi_b, zeta_b = _retention_consts(C)
    row = lambda b, n: b * N + n
    return pl.pallas_call(
        functools.partial(_retention_kernel, gamma_c),
        grid=(B, N),
        in_specs=[
            pl.BlockSpec((C, qk_w), lambda b, n: (row(b, n), 0)),
            pl.BlockSpec((C, qk_w), lambda b, n: (row(b, n), 1)),
            pl.BlockSpec((C, v_w), lambda b, n: (row(b, n), 1)),
            pl.BlockSpec((C, v_w), lambda b, n: (row(b, n), 2)),
            _resident((HEADS, C, C)),
            _resident((HEADS, C, V_DIM)),
            _resident((HEADS, C, QK_DIM)),
            _resident((v_w, D)),
            pl.BlockSpec((C, D), lambda b, n: (row(b, n), 0)),
        ],
        out_specs=pl.BlockSpec((C, D), lambda b, n: (row(b, n), 0)),
        out_shape=jax.ShapeDtypeStruct((T, D), F32),
        scratch_shapes=[pltpu.VMEM((HEADS, QK_DIM, V_DIM), F32), pltpu.VMEM((C, v_w), BF16)],
        compiler_params=_cparams(("parallel", "arbitrary")),
        name="retention",
    )(proj, proj, proj, proj, decay, xi_b, zeta_b, w_out, h)


def _ffn_kernel(h_ref, g_ref, wgu_ref, wd_ref, o_ref):
    h = h_ref[...]
    hn = _rms(h, g_ref[...]).astype(BF16)
    F = wd_ref.shape[0]
    acc = jnp.zeros(h.shape, F32)
    for c0 in range(0, F, FF_CHUNK):
        gate = jnp.dot(hn, wgu_ref[:, c0:c0 + FF_CHUNK], preferred_element_type=F32)
        up = jnp.dot(hn, wgu_ref[:, F + c0:F + c0 + FF_CHUNK], preferred_element_type=F32)
        act = (_silu(gate) * up).astype(BF16)
        acc = acc + jnp.dot(act, wd_ref[c0:c0 + FF_CHUNK, :], preferred_element_type=F32)
    o_ref[...] = h + acc


def _ffn(h, g, w_gu, w_down, tm=512):
    T, D = h.shape
    F = w_down.shape[0]
    return pl.pallas_call(
        _ffn_kernel,
        grid=(T // tm,),
        in_specs=[
            pl.BlockSpec((tm, D), lambda i: (i, 0)),
            _resident((1, D)),
            _resident((D, 2 * F)),
            _resident((F, D)),
        ],
        out_specs=pl.BlockSpec((tm, D), lambda i: (i, 0)),
        out_shape=jax.ShapeDtypeStruct((T, D), F32),
        compiler_params=_cparams(("parallel",)),
        name="dense_swiglu",
    )(h, g, w_gu, w_down)


def _conv_kernel(h_ref, g_ref, w1_ref, b1_ref, wdw_ref, bdw_ref, lng_ref, lnb_ref, w2_ref, b2_ref,
                 o_ref, u_ref, us_ref, c_ref):
    s = pl.program_id(1)
    ts, D = h_ref.shape

    @pl.when(s == 0)
    def _():
        u_ref[0:HALO, :] = jnp.zeros((HALO, D), F32)

    @pl.when(s > 0)
    def _():
        u_ref[0:HALO, :] = u_ref[ts:ts + HALO, :]

    h = h_ref[...]
    hn = _rms(h, g_ref[...]).astype(BF16)
    a = jnp.dot(hn, w1_ref[...], preferred_element_type=F32) + b1_ref[...]
    u_ref[HALO:HALO + ts, :] = a[:, :D] * jax.nn.sigmoid(a[:, D:])

    L = ts + HALO - SUBLANES
    for k in range(1, SUBLANES):
        us_ref[k - 1, 0:L, :] = u_ref[k:k + L, :]

    rc = 32
    base = HALO - (CONV_WIDTH - 1)

    cw = D // 2

    def chunk(r, carry):
        r0 = pl.multiple_of(r * rc, rc)
        for c0 in range(0, D, cw):
            acc = jnp.zeros((rc, cw), F32)
            for j in range(CONV_WIDTH):
                al, ph = divmod(base + j, SUBLANES)
                rows = pl.ds(r0 + al * SUBLANES, rc)
                tap = u_ref[rows, c0:c0 + cw] if ph == 0 else us_ref[ph - 1, rows, c0:c0 + cw]
                acc = acc + tap * wdw_ref[j:j + 1, c0:c0 + cw]
            c_ref[pl.ds(r0, rc), c0:c0 + cw] = acc
        return carry

    lax.fori_loop(0, ts // rc, chunk, 0)

    c = c_ref[...] + bdw_ref[...]
    mu = jnp.mean(c, axis=-1, keepdims=True)
    d = c - mu
    var = jnp.mean(d * d, axis=-1, keepdims=True)
    un = d * lax.rsqrt(var + EPS) * lng_ref[...] + lnb_ref[...]
    act = _silu(un).astype(BF16)
    o_ref[...] = h + jnp.dot(act, w2_ref[...], preferred_element_type=F32) + b2_ref[...]


def _conv_module(h, g, w1, b1, wdw, bdw, lng, lnb, w2, b2, B, S, ts=512):
    T, D = h.shape
    ns = S // ts
    W = wdw.shape[0]
    row = lambda b, s: (b * ns + s, 0)
    return pl.pallas_call(
        _conv_kernel,
        grid=(B, ns),
        in_specs=[
            pl.BlockSpec((ts, D), row),
            _resident((1, D)),
            _resident((D, 2 * D)),
            _resident((1, 2 * D)),
            _resident((W, D)),
            _resident((1, D)),
            _resident((1, D)),
            _resident((1, D)),
            _resident((D, D)),
            _resident((1, D)),
        ],
        out_specs=pl.BlockSpec((ts, D), row),
        out_shape=jax.ShapeDtypeStruct((T, D), F32),
        scratch_shapes=[pltpu.VMEM((ts + HALO, D), F32), pltpu.VMEM((SUBLANES - 1, ts + HALO, D), F32),
                        pltpu.VMEM((ts, D), F32)],
        compiler_params=_cparams(("parallel", "arbitrary")),
        name="conv_module",
    )(h, g, w1, b1, wdw, bdw, lng, lnb, w2, b2)


def _router_kernel(h_ref, g_ref, wr_ref, hn_ref, meta_ref, cnt_ref, run_ref):
    i = pl.program_id(0)
    tm = h_ref.shape[0]

    @pl.when(i == 0)
    def _():
        run_ref[...] = jnp.zeros_like(run_ref)

    hn = _rms(h_ref[...], g_ref[...])
    hi = hn.astype(BF16)
    lo = (hn - hi.astype(F32)).astype(BF16)
    hn_ref[...] = _pack_pairs(hn)
    w = wr_ref[...]
    hw = jnp.dot(hi, w, preferred_element_type=F32)
    logits = hw[:, :LANES] + (hw[:, LANES:] + jnp.dot(lo, w[:, :LANES], preferred_element_type=F32))
    lane = lax.broadcasted_iota(jnp.int32, logits.shape, 1)
    neg = jnp.float32(-jnp.inf)
    logits = jnp.where(lane < N_EXPERTS, logits, neg)
    m1 = jnp.max(logits, axis=-1, keepdims=True)
    i1 = jnp.min(jnp.where(logits == m1, lane, LANES), axis=-1, keepdims=True)
    rest = jnp.where(lane == i1, neg, logits)
    m2 = jnp.max(rest, axis=-1, keepdims=True)
    i2 = jnp.min(jnp.where(rest == m2, lane, LANES), axis=-1, keepdims=True)
    e = jnp.exp(m2 - m1)
    g1 = 1.0 / (1.0 + e)
    g2 = e / (1.0 + e)

    oh1 = (lane == i1).astype(F32)
    oh2 = (lane == i2).astype(F32)
    sel = oh1 + oh2
    r_i = lax.broadcasted_iota(jnp.int32, (tm, tm), 0)
    c_i = lax.broadcasted_iota(jnp.int32, (tm, tm), 1)
    tri = (c_i < r_i).astype(BF16)
    prefix = jnp.dot(tri, sel.astype(BF16), preferred_element_type=F32) + run_ref[...]
    rank1 = jnp.sum(oh1 * prefix, axis=-1, keepdims=True)
    rank2 = jnp.sum(oh2 * prefix, axis=-1, keepdims=True)
    run_ref[...] = run_ref[...] + jnp.sum(sel, axis=0, keepdims=True)
    cnt_ref[...] = jnp.broadcast_to(run_ref[...], cnt_ref.shape)

    meta = jnp.where(lane == 0, i1.astype(F32), 0.0)
    meta = jnp.where(lane == 1, i2.astype(F32), meta)
    meta = jnp.where(lane == 2, rank1, meta)
    meta = jnp.where(lane == 3, rank2, meta)
    meta = jnp.where(lane == 4, g1, meta)
    meta = jnp.where(lane == 5, g2, meta)
    meta_ref[...] = meta


def _router(h, g, wr_pad, tm=512):
    T, D = h.shape
    return pl.pallas_call(
        _router_kernel,
        grid=(T // tm,),
        in_specs=[
            pl.BlockSpec((tm, D), lambda i: (i, 0)),
            _resident((1, D)),
            _resident((D, 2 * LANES)),
        ],
        out_specs=[
            pl.BlockSpec((tm, D // 2), lambda i: (i, 0)),
            pl.BlockSpec((tm, LANES), lambda i: (i, 0)),
            pl.BlockSpec((8, LANES), lambda i: (0, 0)),
        ],
        out_shape=[
            jax.ShapeDtypeStruct((T, D // 2), jnp.int32),
            jax.ShapeDtypeStruct((T, LANES), F32),
            jax.ShapeDtypeStruct((8, LANES), F32),
        ],
        scratch_shapes=[pltpu.VMEM((1, LANES), F32)],
        compiler_params=_cparams(("arbitrary",)),
        name="moe_router",
    )(h, g, wr_pad)


def _sc_gather(table, idx):
    V, D = table.shape
    B = idx.shape[0]
    mesh = plsc.VectorSubcoreMesh(core_axis_name="c", subcore_axis_name="s")
    nw = mesh.num_cores * mesh.num_subcores
    k = SC_CHUNK_BYTES // (D * table.dtype.itemsize)
    b_per_w = B // nw
    n_pairs = b_per_w // (2 * k)
    assert B % (nw * 2 * k) == 0, (B, nw, k)

    @functools.partial(
        pl.kernel, mesh=mesh,
        out_type=jax.ShapeDtypeStruct((B, D), table.dtype),
        scratch_types=[
            pltpu.VMEM((b_per_w,), jnp.int32),
            pltpu.VMEM((k, D), table.dtype),
            pltpu.VMEM((k, D), table.dtype),
            pltpu.SemaphoreType.DMA,
            pltpu.SemaphoreType.DMA,
        ],
    )
    def gather_kernel(table_hbm, idx_hbm, out_hbm, idx_v, buf0, buf1, sem0, sem1):
        wid = lax.axis_index("s") * mesh.num_cores + lax.axis_index("c")
        base = pl.multiple_of(wid * b_per_w, SUBLANES)
        pltpu.sync_copy(idx_hbm.at[pl.ds(base, b_per_w)], idx_v)

        def gather(c, buf, sem):
            rows = idx_v.at[pl.ds(pl.multiple_of(c * k, SUBLANES), k)]
            return pltpu.make_async_copy(table_hbm.at[rows], buf, sem)

        def store(c, buf):
            pltpu.sync_copy(buf, out_hbm.at[pl.ds(pl.multiple_of(base + c * k, SUBLANES), k)])

        gather(0, buf0, sem0).start()

        @pl.loop(0, n_pairs)
        def _(p):
            c = 2 * p
            gather(c + 1, buf1, sem1).start()
            gather(c, buf0, sem0).wait()
            store(c, buf0)

            @pl.when(p + 1 < n_pairs)
            def _():
                gather(c + 2, buf0, sem0).start()

            gather(c + 1, buf1, sem1).wait()
            store(c + 1, buf1)

    return gather_kernel(table, idx)


def _moe_ffn_kernel(te_ref, nv_ref, x_ref, wg_ref, wu_ref, wd_ref, o_ref, xb_ref, acc_ref):
    del te_ref
    i = pl.program_id(0)
    j = pl.program_id(1)
    live = nv_ref[i] > 0

    @pl.when(j == 0)
    def _():
        lo, hi = _unpack_pairs(x_ref[...])
        half = lo.shape[1]
        xb_ref[:, :half] = lo.astype(BF16)
        xb_ref[:, half:] = hi.astype(BF16)
        acc_ref[...] = jnp.zeros_like(acc_ref)

    @pl.when(live)
    def _():
        x = xb_ref[...]
        gate = jnp.dot(x, wg_ref[0], preferred_element_type=F32)
        up = jnp.dot(x, wu_ref[0], preferred_element_type=F32)
        act = (_silu(gate) * up).astype(BF16)
        acc_ref[...] += jnp.dot(act, wd_ref[0], preferred_element_type=F32)

    @pl.when(j == pl.num_programs(1) - 1)
    def _():
        o_ref[...] = _pack_pairs(acc_ref[...])


def _moe_ffn(tile_expert, tile_rows, xs, w_gu, w_down, tm, tf=FF_CHUNK):
    R = xs.shape[0]
    _, F, D = w_down.shape
    nf = F // tf
    grid_spec = pltpu.PrefetchScalarGridSpec(
        num_scalar_prefetch=2,
        grid=(R // tm, nf),
        in_specs=[
            pl.BlockSpec((tm, D // 2), lambda i, j, te, nv: (i, 0)),
            pl.BlockSpec((1, D, tf), lambda i, j, te, nv: (te[i], 0, j)),
            pl.BlockSpec((1, D, tf), lambda i, j, te, nv: (te[i], 0, j + nf)),
            pl.BlockSpec((1, tf, D), lambda i, j, te, nv: (te[i], j, 0)),
        ],
        out_specs=pl.BlockSpec((tm, D // 2), lambda i, j, te, nv: (i, 0)),
        scratch_shapes=[pltpu.VMEM((tm, D), BF16), pltpu.VMEM((tm, D), F32)],
    )
    return pl.pallas_call(
        _moe_ffn_kernel,
        grid_spec=grid_spec,
        out_shape=jax.ShapeDtypeStruct((R, D // 2), jnp.int32),
        compiler_params=_cparams(("arbitrary", "arbitrary")),
        name="moe_swiglu",
    )(tile_expert, tile_rows, xs, w_gu, w_gu, w_down)


def _combine_kernel(h_ref, meta_ref, g_ref, y1_ref, y2_ref, o_ref):
    meta = meta_ref[...]
    g1 = meta[:, 4:5]
    g2 = meta[:, 5:6]
    lo1, hi1 = _unpack_pairs(y1_ref[...])
    lo2, hi2 = _unpack_pairs(y2_ref[...])
    half = lo1.shape[1]
    h = h_ref[...]
    h = jnp.concatenate([h[:, :half] + g1 * lo1 + g2 * lo2, h[:, half:] + g1 * hi1 + g2 * hi2], axis=1)
    o_ref[...] = _rms(h, g_ref[...])


def _combine(h, meta, g, y, tb=512):
    T, D = h.shape
    nb = T // tb
    return pl.pallas_call(
        _combine_kernel,
        grid=(nb,),
        in_specs=[
            pl.BlockSpec((tb, D), lambda i: (i, 0)),
            pl.BlockSpec((tb, LANES), lambda i: (i, 0)),
            _resident((1, D)),
            pl.BlockSpec((tb, D // 2), lambda i: (i, 0)),
            pl.BlockSpec((tb, D // 2), lambda i: (i + nb, 0)),
        ],
        out_specs=pl.BlockSpec((tb, D), lambda i: (i, 0)),
        out_shape=jax.ShapeDtypeStruct((T, D), F32),
        compiler_params=_cparams(("parallel",)),
        name="moe_combine",
    )(h, meta, g, y, y)


def _moe_layer(h, g_ffn, g_final, w_router, w_gu, w_down, tm=MOE_TILE):
    T, D = h.shape
    E = w_router.shape[1]
    wr_pad = jnp.zeros((D, LANES), F32).at[:, :E].set(w_router)
    wr_hi = wr_pad.astype(BF16)
    wr_lo = (wr_pad - wr_hi.astype(F32)).astype(BF16)
    hn, meta, cnt = _router(h, g_ffn, jnp.concatenate([wr_hi, wr_lo], axis=1))

    counts = cnt[0, :E].astype(jnp.int32)
    padded = ((counts + tm - 1) // tm) * tm
    ends = jnp.cumsum(padded)
    offsets = ends - padded
    e1 = meta[:, 0].astype(jnp.int32)
    e2 = meta[:, 1].astype(jnp.int32)
    pos1 = offsets[e1] + meta[:, 2].astype(jnp.int32)
    pos2 = offsets[e2] + meta[:, 3].astype(jnp.int32)
    n_tiles = -(-(2 * T + E * (tm - 1)) // tm)
    R = n_tiles * tm
    tok = jnp.arange(T, dtype=jnp.int32)
    pos = jnp.concatenate([pos1, pos2])
    _, tok_sorted = lax.sort_key_val(pos, jnp.concatenate([tok, tok]))
    tok_sorted = jnp.concatenate([tok_sorted, jnp.zeros((T,), jnp.int32)])
    starts = jnp.cumsum(counts) - counts
    src = jnp.zeros((R + T,), jnp.int32)
    for e in range(E):
        src = lax.dynamic_update_slice(src, lax.dynamic_slice(tok_sorted, (starts[e],), (T,)), (offsets[e],))
    src = src[:R]
    tile_start = jnp.arange(n_tiles, dtype=jnp.int32) * tm
    tile_expert = jnp.minimum(jnp.searchsorted(ends, tile_start, side="right"), E - 1).astype(jnp.int32)
    tile_rows = jnp.clip((offsets + counts)[tile_expert] - tile_start, 0, tm).astype(jnp.int32)

    xs = _sc_gather(hn, src)
    ys = _moe_ffn(tile_expert, tile_rows, xs, w_gu, w_down, tm)
    y = _sc_gather(ys, pos)
    return _combine(h, meta, g_final, y)


def kernel(x, positions, norm_mix_g, norm_ffn_g, final_norm_g, ret_w_in, ret_w_out, conv_w_pw1, conv_b_pw1, conv_w_dw, conv_b_dw, conv_ln_g, conv_ln_b, conv_w_pw2, conv_b_pw2, ffn_w_gu, ffn_w_down, moe_w_router, moe_w_gu, moe_w_down):
    B, S, D = x.shape
    T = B * S
    h = x.reshape(T, D)
    row = lambda v: v.reshape(1, -1)

    posb = jnp.broadcast_to(positions.reshape(T, 1), (T, LANES))
    half = QK_DIM // 2
    inv = (ROPE_BASE ** (-jnp.arange(half, dtype=F32) / half)).reshape(1, half)
    proj = _inproj(h, row(norm_mix_g[0]), posb, inv, ret_w_in[0].astype(BF16))
    h = _retention(proj, ret_w_out[0].astype(BF16), h, B, S)
    h = _ffn(h, row(norm_ffn_g[0]), ffn_w_gu[0].astype(BF16), ffn_w_down[0].astype(BF16))
    h = _conv_module(h, row(norm_mix_g[1]), conv_w_pw1[0].astype(BF16), row(conv_b_pw1[0]),
                     conv_w_dw[0], row(conv_b_dw[0]), row(conv_ln_g[0]), row(conv_ln_b[0]),
                     conv_w_pw2[0].astype(BF16), row(conv_b_pw2[0]), B, S)
    out = _moe_layer(h, row(norm_ffn_g[1]), row(final_norm_g), moe_w_router[0],
                     moe_w_gu[0].astype(BF16), moe_w_down[0].astype(BF16))
    return out.reshape(B, S, D)
```

```python
import functools

import numpy as np
import jax
import jax.numpy as jnp
from jax import lax
from jax.experimental import pallas as pl
from jax.experimental.pallas import tpu as pltpu
from jax.experimental.pallas import tpu_sc as plsc

F32 = jnp.float32
BF16 = jnp.bfloat16

EPS = 1e-6
ROPE_BASE = 10000.0
HEADS = 4
QK_DIM = 256
V_DIM = 512
CONV_WIDTH = 31
N_EXPERTS = 8
LANES = 128
SUBLANES = 8
HALO = 32
VMEM_LIMIT = 56 * 1024 * 1024

RET_CHUNK = 256
FF_CHUNK = 512
MOE_TILE = 1024
SC_CHUNK_BYTES = 128 * 1024


def _cparams(sem):
    return pltpu.CompilerParams(dimension_semantics=sem, vmem_limit_bytes=VMEM_LIMIT)


def _resident(shape):
    zeros = (0,) * len(shape)
    return pl.BlockSpec(shape, lambda *_: zeros, pipeline_mode=pl.Buffered(1))


def _rms(h, g):
    return h * lax.rsqrt(jnp.mean(h * h, axis=-1, keepdims=True) + EPS) * g


def _silu(x):
    return x * jax.nn.sigmoid(x)


def _pack_pairs(x):
    bits = lax.bitcast_convert_type(x.astype(BF16).astype(F32), jnp.uint32)
    half = x.shape[1] // 2
    words = lax.shift_right_logical(bits[:, :half], jnp.uint32(16)) | (bits[:, half:] & jnp.uint32(0xFFFF0000))
    return lax.bitcast_convert_type(words, jnp.int32)


def _unpack_pairs(words):
    w = lax.bitcast_convert_type(words, jnp.uint32)
    lo = lax.bitcast_convert_type(lax.shift_left(w, jnp.uint32(16)), F32)
    hi = lax.bitcast_convert_type(w & jnp.uint32(0xFFFF0000), F32)
    return lo, hi


def _inproj_kernel(h_ref, g_ref, pos_ref, inv_ref, w_ref, o_ref):
    hn = _rms(h_ref[...], g_ref[...]).astype(BF16)
    ang = pos_ref[...].astype(F32) * inv_ref[...]
    c = jnp.cos(ang)
    s = jnp.sin(ang)
    half = QK_DIM // 2
    qk_w = HEADS * QK_DIM
    for hd in range(2 * HEADS):
        col = hd * QK_DIM
        a = jnp.dot(hn, w_ref[:, col:col + QK_DIM], preferred_element_type=F32)
        t1 = a[:, :half]
        t2 = a[:, half:]
        scale = 1.0 if hd < HEADS else QK_DIM ** -0.5
        o_ref[:, col:col + half] = ((t1 * c - t2 * s) * scale).astype(BF16)
        o_ref[:, col + half:col + QK_DIM] = ((t2 * c + t1 * s) * scale).astype(BF16)
    blk = 1024
    for col in range(2 * qk_w, w_ref.shape[1], blk):
        o_ref[:, col:col + blk] = jnp.dot(hn, w_ref[:, col:col + blk], preferred_element_type=F32).astype(BF16)


def _inproj(h, g, posb, inv, w, tm=512):
    T, D = h.shape
    N = w.shape[1]
    return pl.pallas_call(
        _inproj_kernel,
        grid=(T // tm,),
        in_specs=[
            pl.BlockSpec((tm, D), lambda i: (i, 0)),
            _resident((1, D)),
            pl.BlockSpec((tm, LANES), lambda i: (i, 0)),
            _resident((1, LANES)),
            _resident((D, N)),
        ],
        out_specs=pl.BlockSpec((tm, N), lambda i: (i, 0)),
        out_shape=jax.ShapeDtypeStruct((T, N), BF16),
        compiler_params=_cparams(("parallel",)),
        name="ret_inproj",
    )(h, g, posb, inv, w)


def _retention_kernel(gamma_c, q_ref, k_ref, v_ref, g_ref, dec_ref, xi_ref, zeta_ref, wo_ref, h_ref,
                      o_ref, st_ref, gated_ref):
    n = pl.program_id(1)

    @pl.when(n == 0)
    def _():
        st_ref[...] = jnp.zeros_like(st_ref)

    for h in range(HEADS):
        q = q_ref[:, h * QK_DIM:(h + 1) * QK_DIM]
        k = k_ref[:, h * QK_DIM:(h + 1) * QK_DIM]
        v = v_ref[:, h * V_DIM:(h + 1) * V_DIM]
        s = lax.dot_general(q, k, (((1,), (1,)), ((), ())), preferred_element_type=F32)
        s = s * dec_ref[h]
        intra = jnp.dot(s.astype(BF16), v, preferred_element_type=F32)
        state = st_ref[h]
        cross = jnp.dot(q, state.astype(BF16), preferred_element_type=F32) * xi_ref[h]
        o = intra + cross
        kz = (k.astype(F32) * zeta_ref[h]).astype(BF16)
        st_ref[h] = state * gamma_c[h] + lax.dot_general(
            kz, v, (((0,), (0,)), ((), ())), preferred_element_type=F32)
        mu = jnp.mean(o, axis=-1, keepdims=True)
        d = o - mu
        var = jnp.mean(d * d, axis=-1, keepdims=True)
        on = d * lax.rsqrt(var + EPS)
        gate = g_ref[:, h * V_DIM:(h + 1) * V_DIM].astype(F32)
        gated_ref[:, h * V_DIM:(h + 1) * V_DIM] = (_silu(gate) * on).astype(BF16)

    o_ref[...] = h_ref[...] + jnp.dot(gated_ref[...], wo_ref[...], preferred_element_type=F32)


def _retention_consts(C):
    log_gamma = np.log1p(-(2.0 ** (-5.0 - np.arange(HEADS, dtype=np.float64))))
    idx = np.arange(C, dtype=np.float64)
    rel = idx[:, None] - idx[None, :]
    decay = np.where(rel[None] >= 0, np.exp(np.maximum(rel, 0.0)[None] * log_gamma[:, None, None]), 0.0)
    xi = np.exp((idx + 1.0)[None, :] * log_gamma[:, None])
    zeta = np.exp((C - 1.0 - idx)[None, :] * log_gamma[:, None])
    gamma_c = tuple(float(x) for x in np.exp(C * log_gamma))
    xi_b = np.broadcast_to(xi[:, :, None], (HEADS, C, V_DIM))
    zeta_b = np.broadcast_to(zeta[:, :, None], (HEADS, C, QK_DIM))
    return (gamma_c, jnp.asarray(decay, F32), jnp.asarray(xi_b, F32), jnp.asarray(zeta_b, F32))


def _retention(proj, w_out, h, B, S):
    T, D = h.shape
    C = RET_CHUNK
    N = S // C
    qk_w = HEADS * QK_DIM
    v_w = HEADS * V_DIM
    gamma_c, decay, xi_b, zeta_b = _retention_consts(C)
    row = lambda b, n: b * N + n
    return pl.pallas_call(
        functools.partial(_retention_kernel, gamma_c),
        grid=(B, N),
        in_specs=[
            pl.BlockSpec((C, qk_w), lambda b, n: (row(b, n), 0)),
            pl.BlockSpec((C, qk_w), lambda b, n: (row(b, n), 1)),
            pl.BlockSpec((C, v_w), lambda b, n: (row(b, n), 1)),
            pl.BlockSpec((C, v_w), lambda b, n: (row(b, n), 2)),
            _resident((HEADS, C, C)),
            _resident((HEADS, C, V_DIM)),
            _resident((HEADS, C, QK_DIM)),
            _resident((v_w, D)),
            pl.BlockSpec((C, D), lambda b, n: (row(b, n), 0)),
        ],
        out_specs=pl.BlockSpec((C, D), lambda b, n: (row(b, n), 0)),
        out_shape=jax.ShapeDtypeStruct((T, D), F32),
        scratch_shapes=[pltpu.VMEM((HEADS, QK_DIM, V_DIM), F32), pltpu.VMEM((C, v_w), BF16)],
        compiler_params=_cparams(("parallel", "arbitrary")),
        name="retention",
    )(proj, proj, proj, proj, decay, xi_b, zeta_b, w_out, h)


def _ffn_kernel(h_ref, g_ref, wgu_ref, wd_ref, o_ref):
    h = h_ref[...]
    hn = _rms(h, g_ref[...]).astype(BF16)
    F = wd_ref.shape[0]
    acc = jnp.zeros(h.shape, F32)
    for c0 in range(0, F, FF_CHUNK):
        gate = jnp.dot(hn, wgu_ref[:, c0:c0 + FF_CHUNK], preferred_element_type=F32)
        up = jnp.dot(hn, wgu_ref[:, F + c0:F + c0 + FF_CHUNK], preferred_element_type=F32)
        act = (_silu(gate) * up).astype(BF16)
        acc = acc + jnp.dot(act, wd_ref[c0:c0 + FF_CHUNK, :], preferred_element_type=F32)
    o_ref[...] = h + acc


def _ffn(h, g, w_gu, w_down, tm=512):
    T, D = h.shape
    F = w_down.shape[0]
    return pl.pallas_call(
        _ffn_kernel,
        grid=(T // tm,),
        in_specs=[
            pl.BlockSpec((tm, D), lambda i: (i, 0)),
            _resident((1, D)),
            _resident((D, 2 * F)),
            _resident((F, D)),
        ],
        out_specs=pl.BlockSpec((tm, D), lambda i: (i, 0)),
        out_shape=jax.ShapeDtypeStruct((T, D), F32),
        compiler_params=_cparams(("parallel",)),
        name="dense_swiglu",
    )(h, g, w_gu, w_down)


def _conv_kernel(tiles_per_seq, hc_ref, hn_ref, g_ref, w1_ref, b1_ref, wdw_ref, bdw_ref, lng_ref, lnb_ref,
                 w2_ref, b2_ref, o_ref, u_ref, us_ref, c_ref):
    t = pl.program_id(0)
    ts, D = hc_ref.shape
    cur = lax.rem(t, 2)
    prev = 1 - cur

    @pl.when(t == 0)
    def _():
        u_ref[...] = jnp.zeros_like(u_ref)

    L = ts + HALO - SUBLANES
    us_ref[0] = u_ref[prev]
    for k in range(1, SUBLANES):
        us_ref[k, 0:L, :] = u_ref[prev, k:k + L, :]

    rc = 32
    cw = D // 2
    base = HALO - (CONV_WIDTH - 1)
    for r0 in range(0, ts, rc):
        for c0 in range(0, D, cw):
            acc = jnp.zeros((rc, cw), F32)
            for j in range(CONV_WIDTH):
                al, ph = divmod(base + j, SUBLANES)
                row0 = r0 + al * SUBLANES
                acc = acc + us_ref[ph, row0:row0 + rc, c0:c0 + cw] * wdw_ref[j:j + 1, c0:c0 + cw]
            c_ref[r0:r0 + rc, c0:c0 + cw] = acc

    c = c_ref[...] + bdw_ref[...]
    mu = jnp.mean(c, axis=-1, keepdims=True)
    d = c - mu
    var = jnp.mean(d * d, axis=-1, keepdims=True)
    un = d * lax.rsqrt(var + EPS) * lng_ref[...] + lnb_ref[...]
    act = _silu(un).astype(BF16)
    o_ref[...] = hc_ref[...] + jnp.dot(act, w2_ref[...], preferred_element_type=F32) + b2_ref[...]

    first = lax.rem(t, tiles_per_seq) == 0
    u_ref[cur, 0:HALO, :] = jnp.where(first, 0.0, u_ref[prev, ts:ts + HALO, :])
    hn = _rms(hn_ref[...], g_ref[...]).astype(BF16)
    a = jnp.dot(hn, w1_ref[...], preferred_element_type=F32) + b1_ref[...]
    u_ref[cur, HALO:HALO + ts, :] = a[:, :D] * jax.nn.sigmoid(a[:, D:])


def _conv_module(h, g, w1, b1, wdw, bdw, lng, lnb, w2, b2, B, S, ts=512):
    T, D = h.shape
    nt = T // ts
    W = wdw.shape[0]
    conv_tile = lambda t: (jnp.maximum(t - 1, 0), 0)
    glu_tile = lambda t: (jnp.minimum(t, nt - 1), 0)
    return pl.pallas_call(
        functools.partial(_conv_kernel, S // ts),
        grid=(nt + 1,),
        in_specs=[
            pl.BlockSpec((ts, D), conv_tile),
            pl.BlockSpec((ts, D), glu_tile),
            _resident((1, D)),
            _resident((D, 2 * D)),
            _resident((1, 2 * D)),
            _resident((W, D)),
            _resident((1, D)),
            _resident((1, D)),
            _resident((1, D)),
            _resident((D, D)),
            _resident((1, D)),
        ],
        out_specs=pl.BlockSpec((ts, D), conv_tile),
        out_shape=jax.ShapeDtypeStruct((T, D), F32),
        scratch_shapes=[pltpu.VMEM((2, ts + HALO, D), F32), pltpu.VMEM((SUBLANES, ts + HALO, D), F32),
                        pltpu.VMEM((ts, D), F32)],
        compiler_params=_cparams(("arbitrary",)),
        name="conv_module",
    )(h, h, g, w1, b1, wdw, bdw, lng, lnb, w2, b2)


def _router_kernel(h_ref, g_ref, wr_ref, hn_ref, meta_ref, cnt_ref, run_ref):
    i = pl.program_id(0)
    tm = h_ref.shape[0]

    @pl.when(i == 0)
    def _():
        run_ref[...] = jnp.zeros_like(run_ref)

    hn = _rms(h_ref[...], g_ref[...])
    hi = hn.astype(BF16)
    lo = (hn - hi.astype(F32)).astype(BF16)
    hn_ref[...] = _pack_pairs(hn)
    w = wr_ref[...]
    hw = jnp.dot(hi, w, preferred_element_type=F32)
    logits = hw[:, :LANES] + (hw[:, LANES:] + jnp.dot(lo, w[:, :LANES], preferred_element_type=F32))
    lane = lax.broadcasted_iota(jnp.int32, logits.shape, 1)
    neg = jnp.float32(-jnp.inf)
    logits = jnp.where(lane < N_EXPERTS, logits, neg)
    m1 = jnp.max(logits, axis=-1, keepdims=True)
    i1 = jnp.min(jnp.where(logits == m1, lane, LANES), axis=-1, keepdims=True)
    rest = jnp.where(lane == i1, neg, logits)
    m2 = jnp.max(rest, axis=-1, keepdims=True)
    i2 = jnp.min(jnp.where(rest == m2, lane, LANES), axis=-1, keepdims=True)
    e = jnp.exp(m2 - m1)
    g1 = 1.0 / (1.0 + e)
    g2 = e / (1.0 + e)

    oh1 = (lane == i1).astype(F32)
    oh2 = (lane == i2).astype(F32)
    sel = oh1 + oh2
    r_i = lax.broadcasted_iota(jnp.int32, (tm, tm), 0)
    c_i = lax.broadcasted_iota(jnp.int32, (tm, tm), 1)
    tri = (c_i < r_i).astype(BF16)
    prefix = jnp.dot(tri, sel.astype(BF16), preferred_element_type=F32) + run_ref[...]
    rank1 = jnp.sum(oh1 * prefix, axis=-1, keepdims=True)
    rank2 = jnp.sum(oh2 * prefix, axis=-1, keepdims=True)
    run_ref[...] = run_ref[...] + jnp.sum(sel, axis=0, keepdims=True)
    cnt_ref[...] = jnp.broadcast_to(run_ref[...], cnt_ref.shape)

    meta = jnp.where(lane == 0, i1.astype(F32), 0.0)
    meta = jnp.where(lane == 1, i2.astype(F32), meta)
    meta = jnp.where(lane == 2, rank1, meta)
    meta = jnp.where(lane == 3, rank2, meta)
    meta = jnp.where(lane == 4, g1, meta)
    meta = jnp.where(lane == 5, g2, meta)
    meta_ref[...] = meta


def _router(h, g, wr_pad, tm=512):
    T, D = h.shape
    return pl.pallas_call(
        _router_kernel,
        grid=(T // tm,),
        in_specs=[
            pl.BlockSpec((tm, D), lambda i: (i, 0)),
            _resident((1, D)),
            _resident((D, 2 * LANES)),
        ],
        out_specs=[
            pl.BlockSpec((tm, D // 2), lambda i: (i, 0)),
            pl.BlockSpec((tm, LANES), lambda i: (i, 0)),
            pl.BlockSpec((8, LANES), lambda i: (0, 0)),
        ],
        out_shape=[
            jax.ShapeDtypeStruct((T, D // 2), jnp.int32),
            jax.ShapeDtypeStruct((T, LANES), F32),
            jax.ShapeDtypeStruct((8, LANES), F32),
        ],
        scratch_shapes=[pltpu.VMEM((1, LANES), F32)],
        compiler_params=_cparams(("arbitrary",)),
        name="moe_router",
    )(h, g, wr_pad)


def _sc_gather(table, idx):
    V, D = table.shape
    B = idx.shape[0]
    mesh = plsc.VectorSubcoreMesh(core_axis_name="c", subcore_axis_name="s")
    nw = mesh.num_cores * mesh.num_subcores
    k = SC_CHUNK_BYTES // (D * table.dtype.itemsize)
    b_per_w = B // nw
    n_pairs = b_per_w // (2 * k)
    assert B % (nw * 2 * k) == 0, (B, nw, k)

    @functools.partial(
        pl.kernel, mesh=mesh,
        out_type=jax.ShapeDtypeStruct((B, D), table.dtype),
        scratch_types=[
            pltpu.VMEM((b_per_w,), jnp.int32),
            pltpu.VMEM((k, D), table.dtype),
            pltpu.VMEM((k, D), table.dtype),
            pltpu.SemaphoreType.DMA,
            pltpu.SemaphoreType.DMA,
        ],
    )
    def gather_kernel(table_hbm, idx_hbm, out_hbm, idx_v, buf0, buf1, sem0, sem1):
        wid = lax.axis_index("s") * mesh.num_cores + lax.axis_index("c")
        base = pl.multiple_of(wid * b_per_w, SUBLANES)
        pltpu.sync_copy(idx_hbm.at[pl.ds(base, b_per_w)], idx_v)

        def gather(c, buf, sem):
            rows = idx_v.at[pl.ds(pl.multiple_of(c * k, SUBLANES), k)]
            return pltpu.make_async_copy(table_hbm.at[rows], buf, sem)

        def store(c, buf):
            pltpu.sync_copy(buf, out_hbm.at[pl.ds(pl.multiple_of(base + c * k, SUBLANES), k)])

        gather(0, buf0, sem0).start()

        @pl.loop(0, n_pairs)
        def _(p):
            c = 2 * p
            gather(c + 1, buf1, sem1).start()
            gather(c, buf0, sem0).wait()
            store(c, buf0)

            @pl.when(p + 1 < n_pairs)
            def _():
                gather(c + 2, buf0, sem0).start()

            gather(c + 1, buf1, sem1).wait()
            store(c + 1, buf1)

    return gather_kernel(table, idx)


def _moe_ffn_kernel(te_ref, nv_ref, x_ref, wg_ref, wu_ref, wd_ref, o_ref, xb_ref, acc_ref):
    del te_ref
    i = pl.program_id(0)
    j = pl.program_id(1)
    live = nv_ref[i] > 0

    @pl.when(j == 0)
    def _():
        lo, hi = _unpack_pairs(x_ref[...])
        half = lo.shape[1]
        xb_ref[:, :half] = lo.astype(BF16)
        xb_ref[:, half:] = hi.astype(BF16)
        acc_ref[...] = jnp.zeros_like(acc_ref)

    @pl.when(live)
    def _():
        x = xb_ref[...]
        gate = jnp.dot(x, wg_ref[0], preferred_element_type=F32)
        up = jnp.dot(x, wu_ref[0], preferred_element_type=F32)
        act = (_silu(gate) * up).astype(BF16)
        acc_ref[...] += jnp.dot(act, wd_ref[0], preferred_element_type=F32)

    @pl.when(j == pl.num_programs(1) - 1)
    def _():
        o_ref[...] = _pack_pairs(acc_ref[...])


def _moe_ffn(tile_expert, tile_rows, xs, w_gu, w_down, tm, tf=FF_CHUNK):
    R = xs.shape[0]
    _, F, D = w_down.shape
    nf = F // tf
    grid_spec = pltpu.PrefetchScalarGridSpec(
        num_scalar_prefetch=2,
        grid=(R // tm, nf),
        in_specs=[
            pl.BlockSpec((tm, D // 2), lambda i, j, te, nv: (i, 0)),
            pl.BlockSpec((1, D, tf), lambda i, j, te, nv: (te[i], 0, j)),
            pl.BlockSpec((1, D, tf), lambda i, j, te, nv: (te[i], 0, j + nf)),
            pl.BlockSpec((1, tf, D), lambda i, j, te, nv: (te[i], j, 0)),
        ],
        out_specs=pl.BlockSpec((tm, D // 2), lambda i, j, te, nv: (i, 0)),
        scratch_shapes=[pltpu.VMEM((tm, D), BF16), pltpu.VMEM((tm, D), F32)],
    )
    return pl.pallas_call(
        _moe_ffn_kernel,
        grid_spec=grid_spec,
        out_shape=jax.ShapeDtypeStruct((R, D // 2), jnp.int32),
        compiler_params=_cparams(("arbitrary", "arbitrary")),
        name="moe_swiglu",
    )(tile_expert, tile_rows, xs, w_gu, w_gu, w_down)


def _combine_kernel(h_ref, meta_ref, g_ref, y1_ref, y2_ref, o_ref):
    meta = meta_ref[...]
    g1 = meta[:, 4:5]
    g2 = meta[:, 5:6]
    lo1, hi1 = _unpack_pairs(y1_ref[...])
    lo2, hi2 = _unpack_pairs(y2_ref[...])
    half = lo1.shape[1]
    h = h_ref[...]
    h = jnp.concatenate([h[:, :half] + g1 * lo1 + g2 * lo2, h[:, half:] + g1 * hi1 + g2 * hi2], axis=1)
    o_ref[...] = _rms(h, g_ref[...])


def _combine(h, meta, g, y, tb=512):
    T, D = h.shape
    nb = T // tb
    return pl.pallas_call(
        _combine_kernel,
        grid=(nb,),
        in_specs=[
            pl.BlockSpec((tb, D), lambda i: (i, 0)),
            pl.BlockSpec((tb, LANES), lambda i: (i, 0)),
            _resident((1, D)),
            pl.BlockSpec((tb, D // 2), lambda i: (i, 0)),
            pl.BlockSpec((tb, D // 2), lambda i: (i + nb, 0)),
        ],
        out_specs=pl.BlockSpec((tb, D), lambda i: (i, 0)),
        out_shape=jax.ShapeDtypeStruct((T, D), F32),
        compiler_params=_cparams(("parallel",)),
        name="moe_combine",
    )(h, meta, g, y, y)


def _moe_layer(h, g_ffn, g_final, w_router, w_gu, w_down, tm=MOE_TILE):
    T, D = h.shape
    E = w_router.shape[1]
    wr_pad = jnp.zeros((D, LANES), F32).at[:, :E].set(w_router)
    wr_hi = wr_pad.astype(BF16)
    wr_lo = (wr_pad - wr_hi.astype(F32)).astype(BF16)
    hn, meta, cnt = _router(h, g_ffn, jnp.concatenate([wr_hi, wr_lo], axis=1))

    counts = cnt[0, :E].astype(jnp.int32)
    padded = ((counts + tm - 1) // tm) * tm
    ends = jnp.cumsum(padded)
    offsets = ends - padded
    e1 = meta[:, 0].astype(jnp.int32)
    e2 = meta[:, 1].astype(jnp.int32)
    pos1 = offsets[e1] + meta[:, 2].astype(jnp.int32)
    pos2 = offsets[e2] + meta[:, 3].astype(jnp.int32)
    n_tiles = -(-(2 * T + E * (tm - 1)) // tm)
    R = n_tiles * tm
    tok = jnp.arange(T, dtype=jnp.int32)
    pos = jnp.concatenate([pos1, pos2])
    _, tok_sorted = lax.sort_key_val(pos, jnp.concatenate([tok, tok]))
    tok_sorted = jnp.concatenate([tok_sorted, jnp.zeros((T,), jnp.int32)])
    starts = jnp.cumsum(counts) - counts
    src = jnp.zeros((R + T,), jnp.int32)
    for e in range(E):
        src = lax.dynamic_update_slice(src, lax.dynamic_slice(tok_sorted, (starts[e],), (T,)), (offsets[e],))
    src = src[:R]
    tile_start = jnp.arange(n_tiles, dtype=jnp.int32) * tm
    tile_expert = jnp.minimum(jnp.searchsorted(ends, tile_start, side="right"), E - 1).astype(jnp.int32)
    tile_rows = jnp.clip((offsets + counts)[tile_expert] - tile_start, 0, tm).astype(jnp.int32)

    xs = _sc_gather(hn, src)
    ys = _moe_ffn(tile_expert, tile_rows, xs, w_gu, w_down, tm)
    y = _sc_gather(ys, pos)
    return _combine(h, meta, g_final, y)


def kernel(x, positions, norm_mix_g, norm_ffn_g, final_norm_g, ret_w_in, ret_w_out, conv_w_pw1, conv_b_pw1, conv_w_dw, conv_b_dw, conv_ln_g, conv_ln_b, conv_w_pw2, conv_b_pw2, ffn_w_gu, ffn_w_down, moe_w_router, moe_w_gu, moe_w_down):
    B, S, D = x.shape
    T = B * S
    h = x.reshape(T, D)
    row = lambda v: v.reshape(1, -1)

    posb = jnp.broadcast_to(positions.reshape(T, 1), (T, LANES))
    half = QK_DIM // 2
    inv = (ROPE_BASE ** (-jnp.arange(half, dtype=F32) / half)).reshape(1, half)
    proj = _inproj(h, row(norm_mix_g[0]), posb, inv, ret_w_in[0].astype(BF16))
    h = _retention(proj, ret_w_out[0].astype(BF16), h, B, S)
    h = _ffn(h, row(norm_ffn_g[0]), ffn_w_gu[0].astype(BF16), ffn_w_down[0].astype(BF16))
    h = _conv_module(h, row(norm_mix_g[1]), conv_w_pw1[0].astype(BF16), row(conv_b_pw1[0]),
                     conv_w_dw[0], row(conv_b_dw[0]), row(conv_ln_g[0]), row(conv_ln_b[0]),
                     conv_w_pw2[0].astype(BF16), row(conv_b_pw2[0]), B, S)
    out = _moe_layer(h, row(norm_ffn_g[1]), row(final_norm_g), moe_w_router[0],
                     moe_w_gu[0].astype(BF16), moe_w_down[0].astype(BF16))
    return out.reshape(B, S, D)
```

```python
import functools

import numpy as np
import jax
import jax.numpy as jnp
from jax import lax
from jax.experimental import pallas as pl
from jax.experimental.pallas import tpu as pltpu
from jax.experimental.pallas import tpu_sc as plsc

F32 = jnp.float32
BF16 = jnp.bfloat16

EPS = 1e-6
ROPE_BASE = 10000.0
HEADS = 4
QK_DIM = 256
V_DIM = 512
CONV_WIDTH = 31
N_EXPERTS = 8
LANES = 128
SUBLANES = 8
HALO = 32
VMEM_LIMIT = 56 * 1024 * 1024

RET_CHUNK = 256
FF_CHUNK = 512
MOE_TILE = 1024
SC_CHUNK_BYTES = 128 * 1024


def _cparams(sem):
    return pltpu.CompilerParams(dimension_semantics=sem, vmem_limit_bytes=VMEM_LIMIT)


def _resident(shape):
    zeros = (0,) * len(shape)
    return pl.BlockSpec(shape, lambda *_: zeros, pipeline_mode=pl.Buffered(1))


def _rms(h, g):
    return h * lax.rsqrt(jnp.mean(h * h, axis=-1, keepdims=True) + EPS) * g


def _silu(x):
    return x * jax.nn.sigmoid(x)


def _pack_pairs(x):
    bits = lax.bitcast_convert_type(x.astype(BF16).astype(F32), jnp.uint32)
    half = x.shape[1] // 2
    words = lax.shift_right_logical(bits[:, :half], jnp.uint32(16)) | (bits[:, half:] & jnp.uint32(0xFFFF0000))
    return lax.bitcast_convert_type(words, jnp.int32)


def _unpack_pairs(words):
    w = lax.bitcast_convert_type(words, jnp.uint32)
    lo = lax.bitcast_convert_type(lax.shift_left(w, jnp.uint32(16)), F32)
    hi = lax.bitcast_convert_type(w & jnp.uint32(0xFFFF0000), F32)
    return lo, hi


def _inproj_kernel(h_ref, g_ref, pos_ref, inv_ref, w_ref, o_ref):
    hn = _rms(h_ref[...], g_ref[...]).astype(BF16)
    ang = pos_ref[...].astype(F32) * inv_ref[...]
    c = jnp.cos(ang)
    s = jnp.sin(ang)
    half = QK_DIM // 2
    qk_w = HEADS * QK_DIM
    for hd in range(2 * HEADS):
        col = hd * QK_DIM
        a = jnp.dot(hn, w_ref[:, col:col + QK_DIM], preferred_element_type=F32)
        t1 = a[:, :half]
        t2 = a[:, half:]
        scale = 1.0 if hd < HEADS else QK_DIM ** -0.5
        o_ref[:, col:col + half] = ((t1 * c - t2 * s) * scale).astype(BF16)
        o_ref[:, col + half:col + QK_DIM] = ((t2 * c + t1 * s) * scale).astype(BF16)
    blk = 1024
    for col in range(2 * qk_w, w_ref.shape[1], blk):
        o_ref[:, col:col + blk] = jnp.dot(hn, w_ref[:, col:col + blk], preferred_element_type=F32).astype(BF16)


def _inproj(h, g, posb, inv, w, tm=1024):
    T, D = h.shape
    N = w.shape[1]
    return pl.pallas_call(
        _inproj_kernel,
        grid=(T // tm,),
        in_specs=[
            pl.BlockSpec((tm, D), lambda i: (i, 0)),
            _resident((1, D)),
            pl.BlockSpec((tm, LANES), lambda i: (i, 0)),
            _resident((1, LANES)),
            _resident((D, N)),
        ],
        out_specs=pl.BlockSpec((tm, N), lambda i: (i, 0)),
        out_shape=jax.ShapeDtypeStruct((T, N), BF16),
        compiler_params=_cparams(("parallel",)),
        name="ret_inproj",
    )(h, g, posb, inv, w)


def _retention_kernel(gamma_c, q_ref, k_ref, v_ref, g_ref, dec_ref, xi_ref, zeta_ref, wo_ref, h_ref,
                      o_ref, st_ref, gated_ref):
    n = pl.program_id(1)

    @pl.when(n == 0)
    def _():
        st_ref[...] = jnp.zeros_like(st_ref)

    for h in range(HEADS):
        q = q_ref[:, h * QK_DIM:(h + 1) * QK_DIM]
        k = k_ref[:, h * QK_DIM:(h + 1) * QK_DIM]
        v = v_ref[:, h * V_DIM:(h + 1) * V_DIM]
        s = lax.dot_general(q, k, (((1,), (1,)), ((), ())), preferred_element_type=F32)
        s = s * dec_ref[h]
        intra = jnp.dot(s.astype(BF16), v, preferred_element_type=F32)
        state = st_ref[h]
        cross = jnp.dot(q, state.astype(BF16), preferred_element_type=F32) * xi_ref[h]
        o = intra + cross
        kz = (k.astype(F32) * zeta_ref[h]).astype(BF16)
        st_ref[h] = state * gamma_c[h] + lax.dot_general(
            kz, v, (((0,), (0,)), ((), ())), preferred_element_type=F32)
        mu = jnp.mean(o, axis=-1, keepdims=True)
        d = o - mu
        var = jnp.mean(d * d, axis=-1, keepdims=True)
        on = d * lax.rsqrt(var + EPS)
        gate = g_ref[:, h * V_DIM:(h + 1) * V_DIM].astype(F32)
        gated_ref[:, h * V_DIM:(h + 1) * V_DIM] = (_silu(gate) * on).astype(BF16)

    o_ref[...] = h_ref[...] + jnp.dot(gated_ref[...], wo_ref[...], preferred_element_type=F32)


def _retention_consts(C):
    log_gamma = np.log1p(-(2.0 ** (-5.0 - np.arange(HEADS, dtype=np.float64))))
    idx = np.arange(C, dtype=np.float64)
    rel = idx[:, None] - idx[None, :]
    decay = np.where(rel[None] >= 0, np.exp(np.maximum(rel, 0.0)[None] * log_gamma[:, None, None]), 0.0)
    xi = np.exp((idx + 1.0)[None, :] * log_gamma[:, None])
    zeta = np.exp((C - 1.0 - idx)[None, :] * log_gamma[:, None])
    gamma_c = tuple(float(x) for x in np.exp(C * log_gamma))
    xi_b = np.broadcast_to(xi[:, :, None], (HEADS, C, V_DIM))
    zeta_b = np.broadcast_to(zeta[:, :, None], (HEADS, C, QK_DIM))
    return (gamma_c, jnp.asarray(decay, F32), jnp.asarray(xi_b, F32), jnp.asarray(zeta_b, F32))


def _retention(proj, w_out, h, B, S):
    T, D = h.shape
    C = RET_CHUNK
    N = S // C
    qk_w = HEADS * QK_DIM
    v_w = HEADS * V_DIM
    gamma_c, decay, xi_b, zeta_b = _retention_consts(C)
    row = lambda b, n: b * N + n
    return pl.pallas_call(
        functools.partial(_retention_kernel, gamma_c),
        grid=(B, N),
        in_specs=[
            pl.BlockSpec((C, qk_w), lambda b, n: (row(b, n), 0)),
            pl.BlockSpec((C, qk_w), lambda b, n: (row(b, n), 1)),
            pl.BlockSpec((C, v_w), lambda b, n: (row(b, n), 1)),
            pl.BlockSpec((C, v_w), lambda b, n: (row(b, n), 2)),
            _resident((HEADS, C, C)),
            _resident((HEADS, C, V_DIM)),
            _resident((HEADS, C, QK_DIM)),
            _resident((v_w, D)),
            pl.BlockSpec((C, D), lambda b, n: (row(b, n), 0)),
        ],
        out_specs=pl.BlockSpec((C, D), lambda b, n: (row(b, n), 0)),
        out_shape=jax.ShapeDtypeStruct((T, D), F32),
        scratch_shapes=[pltpu.VMEM((HEADS, QK_DIM, V_DIM), F32), pltpu.VMEM((C, v_w), BF16)],
        compiler_params=_cparams(("parallel", "arbitrary")),
        name="retention",
    )(proj, proj, proj, proj, decay, xi_b, zeta_b, w_out, h)


def _ffn_kernel(h_ref, g_ref, wgu_ref, wd_ref, o_ref):
    h = h_ref[...]
    hn = _rms(h, g_ref[...]).astype(BF16)
    F = wd_ref.shape[0]
    acc = jnp.zeros(h.shape, F32)
    for c0 in range(0, F, FF_CHUNK):
        gate = jnp.dot(hn, wgu_ref[:, c0:c0 + FF_CHUNK], preferred_element_type=F32)
        up = jnp.dot(hn, wgu_ref[:, F + c0:F + c0 + FF_CHUNK], preferred_element_type=F32)
        act = (_silu(gate) * up).astype(BF16)
        acc = acc + jnp.dot(act, wd_ref[c0:c0 + FF_CHUNK, :], preferred_element_type=F32)
    o_ref[...] = h + acc


def _ffn(h, g, w_gu, w_down, tm=512):
    T, D = h.shape
    F = w_down.shape[0]
    return pl.pallas_call(
        _ffn_kernel,
        grid=(T // tm,),
        in_specs=[
            pl.BlockSpec((tm, D), lambda i: (i, 0)),
            _resident((1, D)),
            _resident((D, 2 * F)),
            _resident((F, D)),
        ],
        out_specs=pl.BlockSpec((tm, D), lambda i: (i, 0)),
        out_shape=jax.ShapeDtypeStruct((T, D), F32),
        compiler_params=_cparams(("parallel",)),
        name="dense_swiglu",
    )(h, g, w_gu, w_down)


def _conv_kernel(tiles_per_seq, hc_ref, hn_ref, g_ref, w1_ref, b1_ref, wdw_ref, bdw_ref, lng_ref, lnb_ref,
                 w2_ref, b2_ref, o_ref, u_ref, us_ref, c_ref):
    t = pl.program_id(0)
    ts, D = hc_ref.shape
    cur = lax.rem(t, 2)
    prev = 1 - cur

    @pl.when(t == 0)
    def _():
        u_ref[...] = jnp.zeros_like(u_ref)

    L = ts + HALO - SUBLANES
    us_ref[0] = u_ref[prev]
    for k in range(1, SUBLANES):
        us_ref[k, 0:L, :] = u_ref[prev, k:k + L, :]

    rc = 32
    cw = D // 2
    base = HALO - (CONV_WIDTH - 1)
    for r0 in range(0, ts, rc):
        for c0 in range(0, D, cw):
            acc = jnp.zeros((rc, cw), F32)
            for j in range(CONV_WIDTH):
                al, ph = divmod(base + j, SUBLANES)
                row0 = r0 + al * SUBLANES
                acc = acc + us_ref[ph, row0:row0 + rc, c0:c0 + cw] * wdw_ref[j:j + 1, c0:c0 + cw]
            c_ref[r0:r0 + rc, c0:c0 + cw] = acc

    c = c_ref[...] + bdw_ref[...]
    mu = jnp.mean(c, axis=-1, keepdims=True)
    d = c - mu
    var = jnp.mean(d * d, axis=-1, keepdims=True)
    un = d * lax.rsqrt(var + EPS) * lng_ref[...] + lnb_ref[...]
    act = _silu(un).astype(BF16)
    o_ref[...] = hc_ref[...] + jnp.dot(act, w2_ref[...], preferred_element_type=F32) + b2_ref[...]

    first = lax.rem(t, tiles_per_seq) == 0
    u_ref[cur, 0:HALO, :] = jnp.where(first, 0.0, u_ref[prev, ts:ts + HALO, :])
    hn = _rms(hn_ref[...], g_ref[...]).astype(BF16)
    a = jnp.dot(hn, w1_ref[...], preferred_element_type=F32) + b1_ref[...]
    u_ref[cur, HALO:HALO + ts, :] = a[:, :D] * jax.nn.sigmoid(a[:, D:])


def _conv_module(h, g, w1, b1, wdw, bdw, lng, lnb, w2, b2, B, S, ts=512):
    T, D = h.shape
    nt = T // ts
    W = wdw.shape[0]
    conv_tile = lambda t: (jnp.maximum(t - 1, 0), 0)
    glu_tile = lambda t: (jnp.minimum(t, nt - 1), 0)
    return pl.pallas_call(
        functools.partial(_conv_kernel, S // ts),
        grid=(nt + 1,),
        in_specs=[
            pl.BlockSpec((ts, D), conv_tile),
            pl.BlockSpec((ts, D), glu_tile),
            _resident((1, D)),
            _resident((D, 2 * D)),
            _resident((1, 2 * D)),
            _resident((W, D)),
            _resident((1, D)),
            _resident((1, D)),
            _resident((1, D)),
            _resident((D, D)),
            _resident((1, D)),
        ],
        out_specs=pl.BlockSpec((ts, D), conv_tile),
        out_shape=jax.ShapeDtypeStruct((T, D), F32),
        scratch_shapes=[pltpu.VMEM((2, ts + HALO, D), F32), pltpu.VMEM((SUBLANES, ts + HALO, D), F32),
                        pltpu.VMEM((ts, D), F32)],
        compiler_params=_cparams(("arbitrary",)),
        name="conv_module",
    )(h, h, g, w1, b1, wdw, bdw, lng, lnb, w2, b2)


def _router_kernel(h_ref, g_ref, wr_ref, hn_ref, meta_ref, cnt_ref, run_ref):
    i = pl.program_id(0)
    tm = h_ref.shape[0]

    @pl.when(i == 0)
    def _():
        run_ref[...] = jnp.zeros_like(run_ref)

    hn = _rms(h_ref[...], g_ref[...])
    hi = hn.astype(BF16)
    lo = (hn - hi.astype(F32)).astype(BF16)
    hn_ref[...] = _pack_pairs(hn)
    w = wr_ref[...]
    hw = jnp.dot(hi, w, preferred_element_type=F32)
    logits = hw[:, :LANES] + (hw[:, LANES:] + jnp.dot(lo, w[:, :LANES], preferred_element_type=F32))
    lane = lax.broadcasted_iota(jnp.int32, logits.shape, 1)
    neg = jnp.float32(-jnp.inf)
    logits = jnp.where(lane < N_EXPERTS, logits, neg)
    m1 = jnp.max(logits, axis=-1, keepdims=True)
    i1 = jnp.min(jnp.where(logits == m1, lane, LANES), axis=-1, keepdims=True)
    rest = jnp.where(lane == i1, neg, logits)
    m2 = jnp.max(rest, axis=-1, keepdims=True)
    i2 = jnp.min(jnp.where(rest == m2, lane, LANES), axis=-1, keepdims=True)
    e = jnp.exp(m2 - m1)
    g1 = 1.0 / (1.0 + e)
    g2 = e / (1.0 + e)

    oh1 = (lane == i1).astype(F32)
    oh2 = (lane == i2).astype(F32)
    sel = oh1 + oh2
    r_i = lax.broadcasted_iota(jnp.int32, (tm, tm), 0)
    c_i = lax.broadcasted_iota(jnp.int32, (tm, tm), 1)
    tri = (c_i < r_i).astype(BF16)
    prefix = jnp.dot(tri, sel.astype(BF16), preferred_element_type=F32) + run_ref[...]
    rank1 = jnp.sum(oh1 * prefix, axis=-1, keepdims=True)
    rank2 = jnp.sum(oh2 * prefix, axis=-1, keepdims=True)
    run_ref[...] = run_ref[...] + jnp.sum(sel, axis=0, keepdims=True)
    cnt_ref[...] = jnp.broadcast_to(run_ref[...], cnt_ref.shape)

    meta = jnp.where(lane == 0, i1.astype(F32), 0.0)
    meta = jnp.where(lane == 1, i2.astype(F32), meta)
    meta = jnp.where(lane == 2, rank1, meta)
    meta = jnp.where(lane == 3, rank2, meta)
    meta = jnp.where(lane == 4, g1, meta)
    meta = jnp.where(lane == 5, g2, meta)
    meta_ref[...] = meta


def _router(h, g, wr_pad, tm=512):
    T, D = h.shape
    return pl.pallas_call(
        _router_kernel,
        grid=(T // tm,),
        in_specs=[
            pl.BlockSpec((tm, D), lambda i: (i, 0)),
            _resident((1, D)),
            _resident((D, 2 * LANES)),
        ],
        out_specs=[
            pl.BlockSpec((tm, D // 2), lambda i: (i, 0)),
            pl.BlockSpec((tm, LANES), lambda i: (i, 0)),
            pl.BlockSpec((8, LANES), lambda i: (0, 0)),
        ],
        out_shape=[
            jax.ShapeDtypeStruct((T, D // 2), jnp.int32),
            jax.ShapeDtypeStruct((T, LANES), F32),
            jax.ShapeDtypeStruct((8, LANES), F32),
        ],
        scratch_shapes=[pltpu.VMEM((1, LANES), F32)],
        compiler_params=_cparams(("arbitrary",)),
        name="moe_router",
    )(h, g, wr_pad)


def _sc_gather(table, idx):
    V, D = table.shape
    B = idx.shape[0]
    mesh = plsc.VectorSubcoreMesh(core_axis_name="c", subcore_axis_name="s")
    nw = mesh.num_cores * mesh.num_subcores
    k = SC_CHUNK_BYTES // (D * table.dtype.itemsize)
    b_per_w = B // nw
    n_pairs = b_per_w // (2 * k)
    assert B % (nw * 2 * k) == 0, (B, nw, k)

    @functools.partial(
        pl.kernel, mesh=mesh,
        out_type=jax.ShapeDtypeStruct((B, D), table.dtype),
        scratch_types=[
            pltpu.VMEM((b_per_w,), jnp.int32),
            pltpu.VMEM((k, D), table.dtype),
            pltpu.VMEM((k, D), table.dtype),
            pltpu.SemaphoreType.DMA,
            pltpu.SemaphoreType.DMA,
        ],
    )
    def gather_kernel(table_hbm, idx_hbm, out_hbm, idx_v, buf0, buf1, sem0, sem1):
        wid = lax.axis_index("s") * mesh.num_cores + lax.axis_index("c")
        base = pl.multiple_of(wid * b_per_w, SUBLANES)
        pltpu.sync_copy(idx_hbm.at[pl.ds(base, b_per_w)], idx_v)

        def gather(c, buf, sem):
            rows = idx_v.at[pl.ds(pl.multiple_of(c * k, SUBLANES), k)]
            return pltpu.make_async_copy(table_hbm.at[rows], buf, sem)

        def store(c, buf):
            pltpu.sync_copy(buf, out_hbm.at[pl.ds(pl.multiple_of(base + c * k, SUBLANES), k)])

        gather(0, buf0, sem0).start()

        @pl.loop(0, n_pairs)
        def _(p):
            c = 2 * p
            gather(c + 1, buf1, sem1).start()
            gather(c, buf0, sem0).wait()
            store(c, buf0)

            @pl.when(p + 1 < n_pairs)
            def _():
                gather(c + 2, buf0, sem0).start()

            gather(c + 1, buf1, sem1).wait()
            store(c + 1, buf1)

    return gather_kernel(table, idx)


def _moe_ffn_kernel(te_ref, nv_ref, x_ref, wgu_ref, wd_ref, o_ref):
    del te_ref
    i = pl.program_id(0)
    live = nv_ref[i] > 0

    @pl.when(live)
    def _():
        lo, hi = _unpack_pairs(x_ref[...])
        x = jnp.concatenate([lo.astype(BF16), hi.astype(BF16)], axis=1)
        F = wd_ref.shape[1]
        acc = jnp.zeros((x.shape[0], wd_ref.shape[2]), F32)
        for c0 in range(0, F, FF_CHUNK):
            gate = jnp.dot(x, wgu_ref[0, :, c0:c0 + FF_CHUNK], preferred_element_type=F32)
            up = jnp.dot(x, wgu_ref[0, :, F + c0:F + c0 + FF_CHUNK], preferred_element_type=F32)
            act = (_silu(gate) * up).astype(BF16)
            acc = acc + jnp.dot(act, wd_ref[0, c0:c0 + FF_CHUNK, :], preferred_element_type=F32)
        o_ref[...] = _pack_pairs(acc)

    @pl.when(jnp.logical_not(live))
    def _():
        o_ref[...] = jnp.zeros_like(o_ref)


def _moe_ffn(tile_expert, tile_rows, xs, w_gu, w_down, tm):
    R = xs.shape[0]
    _, F, D = w_down.shape
    grid_spec = pltpu.PrefetchScalarGridSpec(
        num_scalar_prefetch=2,
        grid=(R // tm,),
        in_specs=[
            pl.BlockSpec((tm, D // 2), lambda i, te, nv: (i, 0)),
            pl.BlockSpec((1, D, 2 * F), lambda i, te, nv: (te[i], 0, 0), pipeline_mode=pl.Buffered(1)),
            pl.BlockSpec((1, F, D), lambda i, te, nv: (te[i], 0, 0), pipeline_mode=pl.Buffered(1)),
        ],
        out_specs=pl.BlockSpec((tm, D // 2), lambda i, te, nv: (i, 0)),
    )
    return pl.pallas_call(
        _moe_ffn_kernel,
        grid_spec=grid_spec,
        out_shape=jax.ShapeDtypeStruct((R, D // 2), jnp.int32),
        compiler_params=_cparams(("arbitrary",)),
        name="moe_swiglu",
    )(tile_expert, tile_rows, xs, w_gu, w_down)


def _combine_kernel(h_ref, meta_ref, g_ref, y1_ref, y2_ref, o_ref):
    meta = meta_ref[...]
    g1 = meta[:, 4:5]
    g2 = meta[:, 5:6]
    lo1, hi1 = _unpack_pairs(y1_ref[...])
    lo2, hi2 = _unpack_pairs(y2_ref[...])
    half = lo1.shape[1]
    h = h_ref[...]
    h = jnp.concatenate([h[:, :half] + g1 * lo1 + g2 * lo2, h[:, half:] + g1 * hi1 + g2 * hi2], axis=1)
    o_ref[...] = _rms(h, g_ref[...])


def _combine(h, meta, g, y, tb=1024):
    T, D = h.shape
    nb = T // tb
    return pl.pallas_call(
        _combine_kernel,
        grid=(nb,),
        in_specs=[
            pl.BlockSpec((tb, D), lambda i: (i, 0)),
            pl.BlockSpec((tb, LANES), lambda i: (i, 0)),
            _resident((1, D)),
            pl.BlockSpec((tb, D // 2), lambda i: (i, 0)),
            pl.BlockSpec((tb, D // 2), lambda i: (i + nb, 0)),
        ],
        out_specs=pl.BlockSpec((tb, D), lambda i: (i, 0)),
        out_shape=jax.ShapeDtypeStruct((T, D), F32),
        compiler_params=_cparams(("parallel",)),
        name="moe_combine",
    )(h, meta, g, y, y)


def _moe_layer(h, g_ffn, g_final, w_router, w_gu, w_down, tm=MOE_TILE):
    T, D = h.shape
    E = w_router.shape[1]
    wr_pad = jnp.zeros((D, LANES), F32).at[:, :E].set(w_router)
    wr_hi = wr_pad.astype(BF16)
    wr_lo = (wr_pad - wr_hi.astype(F32)).astype(BF16)
    hn, meta, cnt = _router(h, g_ffn, jnp.concatenate([wr_hi, wr_lo], axis=1))

    counts = cnt[0, :E].astype(jnp.int32)
    padded = ((counts + tm - 1) // tm) * tm
    ends = jnp.cumsum(padded)
    offsets = ends - padded
    e1 = meta[:, 0].astype(jnp.int32)
    e2 = meta[:, 1].astype(jnp.int32)
    pos1 = offsets[e1] + meta[:, 2].astype(jnp.int32)
    pos2 = offsets[e2] + meta[:, 3].astype(jnp.int32)
    n_tiles = -(-(2 * T + E * (tm - 1)) // tm)
    R = n_tiles * tm
    tok = jnp.arange(T, dtype=jnp.int32)
    pos = jnp.concatenate([pos1, pos2])
    _, tok_sorted = lax.sort_key_val(pos, jnp.concatenate([tok, tok]))
    tok_sorted = jnp.concatenate([tok_sorted, jnp.zeros((T,), jnp.int32)])
    starts = jnp.cumsum(counts) - counts
    src = jnp.zeros((R + T,), jnp.int32)
    for e in range(E):
        src = lax.dynamic_update_slice(src, lax.dynamic_slice(tok_sorted, (starts[e],), (T,)), (offsets[e],))
    src = src[:R]
    tile_start = jnp.arange(n_tiles, dtype=jnp.int32) * tm
    tile_expert = jnp.minimum(jnp.searchsorted(ends, tile_start, side="right"), E - 1).astype(jnp.int32)
    tile_rows = jnp.clip((offsets + counts)[tile_expert] - tile_start, 0, tm).astype(jnp.int32)

    xs = _sc_gather(hn, src)
    ys = _moe_ffn(tile_expert, tile_rows, xs, w_gu, w_down, tm)
    y = _sc_gather(ys, pos)
    return _combine(h, meta, g_final, y)


def kernel(x, positions, norm_mix_g, norm_ffn_g, final_norm_g, ret_w_in, ret_w_out, conv_w_pw1, conv_b_pw1, conv_w_dw, conv_b_dw, conv_ln_g, conv_ln_b, conv_w_pw2, conv_b_pw2, ffn_w_gu, ffn_w_down, moe_w_router, moe_w_gu, moe_w_down):
    B, S, D = x.shape
    T = B * S
    h = x.reshape(T, D)
    row = lambda v: v.reshape(1, -1)

    posb = jnp.broadcast_to(positions.reshape(T, 1), (T, LANES))
    half = QK_DIM // 2
    inv = (ROPE_BASE ** (-jnp.arange(half, dtype=F32) / half)).reshape(1, half)
    proj = _inproj(h, row(norm_mix_g[0]), posb, inv, ret_w_in[0].astype(BF16))
    h = _retention(proj, ret_w_out[0].astype(BF16), h, B, S)
    h = _ffn(h, row(norm_ffn_g[0]), ffn_w_gu[0].astype(BF16), ffn_w_down[0].astype(BF16))
    h = _conv_module(h, row(norm_mix_g[1]), conv_w_pw1[0].astype(BF16), row(conv_b_pw1[0]),
                     conv_w_dw[0], row(conv_b_dw[0]), row(conv_ln_g[0]), row(conv_ln_b[0]),
                     conv_w_pw2[0].astype(BF16), row(conv_b_pw2[0]), B, S)
    out = _moe_layer(h, row(norm_ffn_g[1]), row(final_norm_g), moe_w_router[0],
                     moe_w_gu[0].astype(BF16), moe_w_down[0].astype(BF16))
    return out.reshape(B, S, D)
```

```python
import functools

import numpy as np
import jax
import jax.numpy as jnp
from jax import lax
from jax.experimental import pallas as pl
from jax.experimental.pallas import tpu as pltpu
from jax.experimental.pallas import tpu_sc as plsc

F32 = jnp.float32
BF16 = jnp.bfloat16

EPS = 1e-6
ROPE_BASE = 10000.0
HEADS = 4
QK_DIM = 256
V_DIM = 512
CONV_WIDTH = 31
N_EXPERTS = 8
LANES = 128
SUBLANES = 8
HALO = 32
VMEM_LIMIT = 56 * 1024 * 1024

RET_CHUNK = 256
FF_CHUNK = 512
MOE_TILE = 1024
SC_CHUNK_BYTES = 128 * 1024


def _cparams(sem):
    return pltpu.CompilerParams(dimension_semantics=sem, vmem_limit_bytes=VMEM_LIMIT)


def _resident(shape):
    zeros = (0,) * len(shape)
    return pl.BlockSpec(shape, lambda *_: zeros, pipeline_mode=pl.Buffered(1))


def _rms(h, g):
    return h * lax.rsqrt(jnp.mean(h * h, axis=-1, keepdims=True) + EPS) * g


def _silu(x):
    return x * jax.nn.sigmoid(x)


def _pack_pairs(x):
    bits = lax.bitcast_convert_type(x.astype(BF16).astype(F32), jnp.uint32)
    half = x.shape[1] // 2
    words = lax.shift_right_logical(bits[:, :half], jnp.uint32(16)) | (bits[:, half:] & jnp.uint32(0xFFFF0000))
    return lax.bitcast_convert_type(words, jnp.int32)


def _unpack_pairs(words):
    w = lax.bitcast_convert_type(words, jnp.uint32)
    lo = lax.bitcast_convert_type(lax.shift_left(w, jnp.uint32(16)), F32)
    hi = lax.bitcast_convert_type(w & jnp.uint32(0xFFFF0000), F32)
    return lo, hi


def _inproj_kernel(h_ref, g_ref, pos_ref, inv_ref, w_ref, o_ref):
    hn = _rms(h_ref[...], g_ref[...]).astype(BF16)
    ang = pos_ref[...].astype(F32) * inv_ref[...]
    c = jnp.cos(ang)
    s = jnp.sin(ang)
    half = QK_DIM // 2
    qk_w = HEADS * QK_DIM
    for hd in range(2 * HEADS):
        col = hd * QK_DIM
        a = jnp.dot(hn, w_ref[:, col:col + QK_DIM], preferred_element_type=F32)
        t1 = a[:, :half]
        t2 = a[:, half:]
        scale = 1.0 if hd < HEADS else QK_DIM ** -0.5
        o_ref[:, col:col + half] = ((t1 * c - t2 * s) * scale).astype(BF16)
        o_ref[:, col + half:col + QK_DIM] = ((t2 * c + t1 * s) * scale).astype(BF16)
    blk = 1024
    for col in range(2 * qk_w, w_ref.shape[1], blk):
        o_ref[:, col:col + blk] = jnp.dot(hn, w_ref[:, col:col + blk], preferred_element_type=F32).astype(BF16)


def _inproj(h, g, posb, inv, w, tm=1024):
    T, D = h.shape
    N = w.shape[1]
    return pl.pallas_call(
        _inproj_kernel,
        grid=(T // tm,),
        in_specs=[
            pl.BlockSpec((tm, D), lambda i: (i, 0)),
            _resident((1, D)),
            pl.BlockSpec((tm, LANES), lambda i: (i, 0)),
            _resident((1, LANES)),
            _resident((D, N)),
        ],
        out_specs=pl.BlockSpec((tm, N), lambda i: (i, 0)),
        out_shape=jax.ShapeDtypeStruct((T, N), BF16),
        compiler_params=_cparams(("parallel",)),
        name="ret_inproj",
    )(h, g, posb, inv, w)


def _retention_kernel(gamma_c, q_ref, k_ref, v_ref, g_ref, dec_ref, xi_ref, zeta_ref, wo_ref, h_ref,
                      o_ref, st_ref, gated_ref):
    n = pl.program_id(1)

    @pl.when(n == 0)
    def _():
        st_ref[...] = jnp.zeros_like(st_ref)

    for h in range(HEADS):
        q = q_ref[:, h * QK_DIM:(h + 1) * QK_DIM]
        k = k_ref[:, h * QK_DIM:(h + 1) * QK_DIM]
        v = v_ref[:, h * V_DIM:(h + 1) * V_DIM]
        s = lax.dot_general(q, k, (((1,), (1,)), ((), ())), preferred_element_type=F32)
        s = s * dec_ref[h]
        intra = jnp.dot(s.astype(BF16), v, preferred_element_type=F32)
        state = st_ref[h]
        cross = jnp.dot(q, state.astype(BF16), preferred_element_type=F32) * xi_ref[h]
        o = intra + cross
        kz = (k.astype(F32) * zeta_ref[h]).astype(BF16)
        st_ref[h] = state * gamma_c[h] + lax.dot_general(
            kz, v, (((0,), (0,)), ((), ())), preferred_element_type=F32)
        mu = jnp.mean(o, axis=-1, keepdims=True)
        d = o - mu
        var = jnp.mean(d * d, axis=-1, keepdims=True)
        on = d * lax.rsqrt(var + EPS)
        gate = g_ref[:, h * V_DIM:(h + 1) * V_DIM].astype(F32)
        gated_ref[:, h * V_DIM:(h + 1) * V_DIM] = (_silu(gate) * on).astype(BF16)

    o_ref[...] = h_ref[...] + jnp.dot(gated_ref[...], wo_ref[...], preferred_element_type=F32)


def _retention_consts(C):
    log_gamma = np.log1p(-(2.0 ** (-5.0 - np.arange(HEADS, dtype=np.float64))))
    idx = np.arange(C, dtype=np.float64)
    rel = idx[:, None] - idx[None, :]
    decay = np.where(rel[None] >= 0, np.exp(np.maximum(rel, 0.0)[None] * log_gamma[:, None, None]), 0.0)
    xi = np.exp((idx + 1.0)[None, :] * log_gamma[:, None])
    zeta = np.exp((C - 1.0 - idx)[None, :] * log_gamma[:, None])
    gamma_c = tuple(float(x) for x in np.exp(C * log_gamma))
    xi_b = np.broadcast_to(xi[:, :, None], (HEADS, C, V_DIM))
    zeta_b = np.broadcast_to(zeta[:, :, None], (HEADS, C, QK_DIM))
    return (gamma_c, jnp.asarray(decay, F32), jnp.asarray(xi_b, F32), jnp.asarray(zeta_b, F32))


def _retention(proj, w_out, h, B, S):
    T, D = h.shape
    C = RET_CHUNK
    N = S // C
    qk_w = HEADS * QK_DIM
    v_w = HEADS * V_DIM
    gamma_c, decay, xi_b, zeta_b = _retention_consts(C)
    row = lambda b, n: b * N + n
    return pl.pallas_call(
        functools.partial(_retention_kernel, gamma_c),
        grid=(B, N),
        in_specs=[
            pl.BlockSpec((C, qk_w), lambda b, n: (row(b, n), 0)),
            pl.BlockSpec((C, qk_w), lambda b, n: (row(b, n), 1)),
            pl.BlockSpec((C, v_w), lambda b, n: (row(b, n), 1)),
            pl.BlockSpec((C, v_w), lambda b, n: (row(b, n), 2)),
            _resident((HEADS, C, C)),
            _resident((HEADS, C, V_DIM)),
            _resident((HEADS, C, QK_DIM)),
            _resident((v_w, D)),
            pl.BlockSpec((C, D), lambda b, n: (row(b, n), 0)),
        ],
        out_specs=pl.BlockSpec((C, D), lambda b, n: (row(b, n), 0)),
        out_shape=jax.ShapeDtypeStruct((T, D), F32),
        scratch_shapes=[pltpu.VMEM((HEADS, QK_DIM, V_DIM), F32), pltpu.VMEM((C, v_w), BF16)],
        compiler_params=_cparams(("parallel", "arbitrary")),
        name="retention",
    )(proj, proj, proj, proj, decay, xi_b, zeta_b, w_out, h)


def _ffn_kernel(h_ref, g_ref, wgu_ref, wd_ref, o_ref):
    h = h_ref[...]
    hn = _rms(h, g_ref[...]).astype(BF16)
    F = wd_ref.shape[0]
    acc = jnp.zeros(h.shape, F32)
    for c0 in range(0, F, FF_CHUNK):
        gate = jnp.dot(hn, wgu_ref[:, c0:c0 + FF_CHUNK], preferred_element_type=F32)
        up = jnp.dot(hn, wgu_ref[:, F + c0:F + c0 + FF_CHUNK], preferred_element_type=F32)
        act = (_silu(gate) * up).astype(BF16)
        acc = acc + jnp.dot(act, wd_ref[c0:c0 + FF_CHUNK, :], preferred_element_type=F32)
    o_ref[...] = h + acc


def _ffn(h, g, w_gu, w_down, tm=512):
    T, D = h.shape
    F = w_down.shape[0]
    return pl.pallas_call(
        _ffn_kernel,
        grid=(T // tm,),
        in_specs=[
            pl.BlockSpec((tm, D), lambda i: (i, 0)),
            _resident((1, D)),
            _resident((D, 2 * F)),
            _resident((F, D)),
        ],
        out_specs=pl.BlockSpec((tm, D), lambda i: (i, 0)),
        out_shape=jax.ShapeDtypeStruct((T, D), F32),
        compiler_params=_cparams(("parallel",)),
        name="dense_swiglu",
    )(h, g, w_gu, w_down)


def _conv_kernel(tiles_per_seq, hc_ref, hn_ref, g_ref, w1_ref, b1_ref, wdw_ref, bdw_ref, lng_ref, lnb_ref,
                 w2_ref, b2_ref, o_ref, u_ref, us_ref, c_ref):
    t = pl.program_id(0)
    ts, D = hc_ref.shape
    cur = lax.rem(t, 2)
    prev = 1 - cur

    @pl.when(t == 0)
    def _():
        u_ref[...] = jnp.zeros_like(u_ref)

    L = ts + HALO - SUBLANES
    us_ref[0] = u_ref[prev]
    for k in range(1, SUBLANES):
        us_ref[k, 0:L, :] = u_ref[prev, k:k + L, :]

    rc = 32
    cw = D // 2
    base = HALO - (CONV_WIDTH - 1)
    for r0 in range(0, ts, rc):
        for c0 in range(0, D, cw):
            acc = jnp.zeros((rc, cw), F32)
            for j in range(CONV_WIDTH):
                al, ph = divmod(base + j, SUBLANES)
                row0 = r0 + al * SUBLANES
                acc = acc + us_ref[ph, row0:row0 + rc, c0:c0 + cw] * wdw_ref[j:j + 1, c0:c0 + cw]
            c_ref[r0:r0 + rc, c0:c0 + cw] = acc

    c = c_ref[...] + bdw_ref[...]
    mu = jnp.mean(c, axis=-1, keepdims=True)
    d = c - mu
    var = jnp.mean(d * d, axis=-1, keepdims=True)
    un = d * lax.rsqrt(var + EPS) * lng_ref[...] + lnb_ref[...]
    act = _silu(un).astype(BF16)
    o_ref[...] = hc_ref[...] + jnp.dot(act, w2_ref[...], preferred_element_type=F32) + b2_ref[...]

    first = lax.rem(t, tiles_per_seq) == 0
    u_ref[cur, 0:HALO, :] = jnp.where(first, 0.0, u_ref[prev, ts:ts + HALO, :])
    hn = _rms(hn_ref[...], g_ref[...]).astype(BF16)
    a = jnp.dot(hn, w1_ref[...], preferred_element_type=F32) + b1_ref[...]
    u_ref[cur, HALO:HALO + ts, :] = a[:, :D] * jax.nn.sigmoid(a[:, D:])


def _conv_module(h, g, w1, b1, wdw, bdw, lng, lnb, w2, b2, B, S, ts=512):
    T, D = h.shape
    nt = T // ts
    W = wdw.shape[0]
    conv_tile = lambda t: (jnp.maximum(t - 1, 0), 0)
    glu_tile = lambda t: (jnp.minimum(t, nt - 1), 0)
    return pl.pallas_call(
        functools.partial(_conv_kernel, S // ts),
        grid=(nt + 1,),
        in_specs=[
            pl.BlockSpec((ts, D), conv_tile),
            pl.BlockSpec((ts, D), glu_tile),
            _resident((1, D)),
            _resident((D, 2 * D)),
            _resident((1, 2 * D)),
            _resident((W, D)),
            _resident((1, D)),
            _resident((1, D)),
            _resident((1, D)),
            _resident((D, D)),
            _resident((1, D)),
        ],
        out_specs=pl.BlockSpec((ts, D), conv_tile),
        out_shape=jax.ShapeDtypeStruct((T, D), F32),
        scratch_shapes=[pltpu.VMEM((2, ts + HALO, D), F32), pltpu.VMEM((SUBLANES, ts + HALO, D), F32),
                        pltpu.VMEM((ts, D), F32)],
        compiler_params=_cparams(("arbitrary",)),
        name="conv_module",
    )(h, h, g, w1, b1, wdw, bdw, lng, lnb, w2, b2)


def _router_kernel(h_ref, g_ref, wr_ref, hn_ref, meta_ref, metat_ref, cnt_ref, run_ref):
    i = pl.program_id(0)
    tm = h_ref.shape[0]

    @pl.when(i == 0)
    def _():
        run_ref[...] = jnp.zeros_like(run_ref)

    hn = _rms(h_ref[...], g_ref[...])
    hi = hn.astype(BF16)
    lo = (hn - hi.astype(F32)).astype(BF16)
    hn_ref[...] = _pack_pairs(hn)
    w = wr_ref[...]
    hw = jnp.dot(hi, w, preferred_element_type=F32)
    logits = hw[:, :LANES] + (hw[:, LANES:] + jnp.dot(lo, w[:, :LANES], preferred_element_type=F32))
    lane = lax.broadcasted_iota(jnp.int32, logits.shape, 1)
    neg = jnp.float32(-jnp.inf)
    logits = jnp.where(lane < N_EXPERTS, logits, neg)
    m1 = jnp.max(logits, axis=-1, keepdims=True)
    i1 = jnp.min(jnp.where(logits == m1, lane, LANES), axis=-1, keepdims=True)
    rest = jnp.where(lane == i1, neg, logits)
    m2 = jnp.max(rest, axis=-1, keepdims=True)
    i2 = jnp.min(jnp.where(rest == m2, lane, LANES), axis=-1, keepdims=True)
    e = jnp.exp(m2 - m1)
    g1 = 1.0 / (1.0 + e)
    g2 = e / (1.0 + e)

    oh1 = (lane == i1).astype(F32)
    oh2 = (lane == i2).astype(F32)
    sel = oh1 + oh2
    r_i = lax.broadcasted_iota(jnp.int32, (tm, tm), 0)
    c_i = lax.broadcasted_iota(jnp.int32, (tm, tm), 1)
    tri = (c_i < r_i).astype(BF16)
    prefix = jnp.dot(tri, sel.astype(BF16), preferred_element_type=F32) + run_ref[...]
    rank1 = jnp.sum(oh1 * prefix, axis=-1, keepdims=True)
    rank2 = jnp.sum(oh2 * prefix, axis=-1, keepdims=True)
    run_ref[...] = run_ref[...] + jnp.sum(sel, axis=0, keepdims=True)
    cnt_ref[...] = jnp.broadcast_to(run_ref[...], cnt_ref.shape)

    meta = jnp.where(lane == 0, i1.astype(F32), 0.0)
    meta = jnp.where(lane == 1, i2.astype(F32), meta)
    meta = jnp.where(lane == 2, rank1, meta)
    meta = jnp.where(lane == 3, rank2, meta)
    meta = jnp.where(lane == 4, g1, meta)
    meta = jnp.where(lane == 5, g2, meta)
    meta_ref[...] = meta
    metat_ref[...] = meta.T[:SUBLANES, :]


def _router(h, g, wr_pad, tm=512):
    T, D = h.shape
    return pl.pallas_call(
        _router_kernel,
        grid=(T // tm,),
        in_specs=[
            pl.BlockSpec((tm, D), lambda i: (i, 0)),
            _resident((1, D)),
            _resident((D, 2 * LANES)),
        ],
        out_specs=[
            pl.BlockSpec((tm, D // 2), lambda i: (i, 0)),
            pl.BlockSpec((tm, LANES), lambda i: (i, 0)),
            pl.BlockSpec((SUBLANES, tm), lambda i: (0, i)),
            pl.BlockSpec((SUBLANES, LANES), lambda i: (0, 0)),
        ],
        out_shape=[
            jax.ShapeDtypeStruct((T, D // 2), jnp.int32),
            jax.ShapeDtypeStruct((T, LANES), F32),
            jax.ShapeDtypeStruct((SUBLANES, T), F32),
            jax.ShapeDtypeStruct((SUBLANES, LANES), F32),
        ],
        scratch_shapes=[pltpu.VMEM((1, LANES), F32)],
        compiler_params=_cparams(("arbitrary",)),
        name="moe_router",
    )(h, g, wr_pad)


def _sc_gather(table, idx):
    D = table.shape[1]
    B = idx.shape[0]
    mesh = plsc.VectorSubcoreMesh(core_axis_name="c", subcore_axis_name="s")
    nw = mesh.num_cores * mesh.num_subcores
    k = SC_CHUNK_BYTES // (D * table.dtype.itemsize)
    b_per_w = B // nw
    n_pairs = b_per_w // (2 * k)
    assert B % (nw * 2 * k) == 0, (B, nw, k)

    @functools.partial(
        pl.kernel, mesh=mesh,
        out_type=jax.ShapeDtypeStruct((B, D), table.dtype),
        scratch_types=[
            pltpu.VMEM((b_per_w,), jnp.int32),
            pltpu.VMEM((k, D), table.dtype),
            pltpu.VMEM((k, D), table.dtype),
            pltpu.SemaphoreType.DMA,
            pltpu.SemaphoreType.DMA,
        ],
    )
    def gather_kernel(table_hbm, idx_hbm, out_hbm, idx_v, buf0, buf1, sem0, sem1):
        wid = lax.axis_index("s") * mesh.num_cores + lax.axis_index("c")
        base = pl.multiple_of(wid * b_per_w, SUBLANES)
        pltpu.sync_copy(idx_hbm.at[pl.ds(base, b_per_w)], idx_v)

        def gather(c, buf, sem):
            rows = idx_v.at[pl.ds(pl.multiple_of(c * k, SUBLANES), k)]
            return pltpu.make_async_copy(table_hbm.at[rows], buf, sem)

        def store(c, buf):
            pltpu.sync_copy(buf, out_hbm.at[pl.ds(pl.multiple_of(base + c * k, SUBLANES), k)])

        gather(0, buf0, sem0).start()

        @pl.loop(0, n_pairs)
        def _(p):
            c = 2 * p
            gather(c + 1, buf1, sem1).start()
            gather(c, buf0, sem0).wait()
            store(c, buf0)

            @pl.when(p + 1 < n_pairs)
            def _():
                gather(c + 2, buf0, sem0).start()

            gather(c + 1, buf1, sem1).wait()
            store(c + 1, buf1)

    return gather_kernel(table, idx)


def _moe_ffn_kernel(te_ref, nv_ref, x_ref, wgu_ref, wd_ref, o_ref):
    del te_ref
    i = pl.program_id(0)
    live = nv_ref[i] > 0

    @pl.when(live)
    def _():
        lo, hi = _unpack_pairs(x_ref[...])
        x = jnp.concatenate([lo.astype(BF16), hi.astype(BF16)], axis=1)
        F = wd_ref.shape[1]
        acc = jnp.zeros((x.shape[0], wd_ref.shape[2]), F32)
        for c0 in range(0, F, FF_CHUNK):
            gate = jnp.dot(x, wgu_ref[0, :, c0:c0 + FF_CHUNK], preferred_element_type=F32)
            up = jnp.dot(x, wgu_ref[0, :, F + c0:F + c0 + FF_CHUNK], preferred_element_type=F32)
            act = (_silu(gate) * up).astype(BF16)
            acc = acc + jnp.dot(act, wd_ref[0, c0:c0 + FF_CHUNK, :], preferred_element_type=F32)
        o_ref[...] = _pack_pairs(acc)

    @pl.when(jnp.logical_not(live))
    def _():
        o_ref[...] = jnp.zeros_like(o_ref)


def _moe_ffn(tile_expert, tile_rows, xs, w_gu, w_down, tm):
    R = xs.shape[0]
    _, F, D = w_down.shape
    grid_spec = pltpu.PrefetchScalarGridSpec(
        num_scalar_prefetch=2,
        grid=(R // tm,),
        in_specs=[
            pl.BlockSpec((tm, D // 2), lambda i, te, nv: (i, 0)),
            pl.BlockSpec((1, D, 2 * F), lambda i, te, nv: (te[i], 0, 0), pipeline_mode=pl.Buffered(1)),
            pl.BlockSpec((1, F, D), lambda i, te, nv: (te[i], 0, 0), pipeline_mode=pl.Buffered(1)),
        ],
        out_specs=pl.BlockSpec((tm, D // 2), lambda i, te, nv: (i, 0)),
    )
    return pl.pallas_call(
        _moe_ffn_kernel,
        grid_spec=grid_spec,
        out_shape=jax.ShapeDtypeStruct((R, D // 2), jnp.int32),
        compiler_params=_cparams(("arbitrary",)),
        name="moe_swiglu",
    )(tile_expert, tile_rows, xs, w_gu, w_down)


def _combine_kernel(h_ref, meta_ref, g_ref, y1_ref, y2_ref, o_ref):
    meta = meta_ref[...]
    g1 = meta[:, 4:5]
    g2 = meta[:, 5:6]
    lo1, hi1 = _unpack_pairs(y1_ref[...])
    lo2, hi2 = _unpack_pairs(y2_ref[...])
    half = lo1.shape[1]
    h = h_ref[...]
    h = jnp.concatenate([h[:, :half] + g1 * lo1 + g2 * lo2, h[:, half:] + g1 * hi1 + g2 * hi2], axis=1)
    o_ref[...] = _rms(h, g_ref[...])


def _combine(h, meta, g, y, tb=1024):
    T, D = h.shape
    nb = T // tb
    return pl.pallas_call(
        _combine_kernel,
        grid=(nb,),
        in_specs=[
            pl.BlockSpec((tb, D), lambda i: (i, 0)),
            pl.BlockSpec((tb, LANES), lambda i: (i, 0)),
            _resident((1, D)),
            pl.BlockSpec((tb, D // 2), lambda i: (i, 0)),
            pl.BlockSpec((tb, D // 2), lambda i: (i + nb, 0)),
        ],
        out_specs=pl.BlockSpec((tb, D), lambda i: (i, 0)),
        out_shape=jax.ShapeDtypeStruct((T, D), F32),
        compiler_params=_cparams(("parallel",)),
        name="moe_combine",
    )(h, meta, g, y, y)


def _moe_layer(h, g_ffn, g_final, w_router, w_gu, w_down, tm=MOE_TILE):
    T, D = h.shape
    E = w_router.shape[1]
    wr_pad = jnp.zeros((D, LANES), F32).at[:, :E].set(w_router)
    wr_hi = wr_pad.astype(BF16)
    wr_lo = (wr_pad - wr_hi.astype(F32)).astype(BF16)
    hn, meta, meta_t, cnt = _router(h, g_ffn, jnp.concatenate([wr_hi, wr_lo], axis=1))

    counts = cnt[0, :E].astype(jnp.int32)
    padded = ((counts + tm - 1) // tm) * tm
    ends = jnp.cumsum(padded)
    offsets = ends - padded
    route = meta_t.astype(jnp.int32)
    pos = jnp.concatenate([offsets[route[0]] + route[2], offsets[route[1]] + route[3]])
    n_tiles = -(-(2 * T + E * (tm - 1)) // tm)
    R = n_tiles * tm
    tok = jnp.arange(T, dtype=jnp.int32)
    _, tok_sorted = lax.sort_key_val(pos, jnp.concatenate([tok, tok]))
    tok_sorted = jnp.concatenate([tok_sorted, jnp.zeros((T,), jnp.int32)])
    starts = jnp.cumsum(counts) - counts
    src = jnp.zeros((R + T,), jnp.int32)
    for e in range(E):
        src = lax.dynamic_update_slice(src, lax.dynamic_slice(tok_sorted, (starts[e],), (T,)), (offsets[e],))
    src = src[:R]
    tile_start = jnp.arange(n_tiles, dtype=jnp.int32) * tm
    tile_expert = jnp.minimum(jnp.sum(tile_start[:, None] >= ends[None, :], axis=1), E - 1).astype(jnp.int32)
    tile_rows = jnp.clip((offsets + counts)[tile_expert] - tile_start, 0, tm).astype(jnp.int32)

    xs = _sc_gather(hn, src)
    ys = _moe_ffn(tile_expert, tile_rows, xs, w_gu, w_down, tm)
    y = _sc_gather(ys, pos)
    return _combine(h, meta, g_final, y)


def kernel(x, positions, norm_mix_g, norm_ffn_g, final_norm_g, ret_w_in, ret_w_out, conv_w_pw1, conv_b_pw1, conv_w_dw, conv_b_dw, conv_ln_g, conv_ln_b, conv_w_pw2, conv_b_pw2, ffn_w_gu, ffn_w_down, moe_w_router, moe_w_gu, moe_w_down):
    B, S, D = x.shape
    T = B * S
    h = x.reshape(T, D)
    row = lambda v: v.reshape(1, -1)

    posb = jnp.broadcast_to(positions.reshape(T, 1), (T, LANES))
    half = QK_DIM // 2
    inv = (ROPE_BASE ** (-jnp.arange(half, dtype=F32) / half)).reshape(1, half)
    proj = _inproj(h, row(norm_mix_g[0]), posb, inv, ret_w_in[0].astype(BF16))
    h = _retention(proj, ret_w_out[0].astype(BF16), h, B, S)
    h = _ffn(h, row(norm_ffn_g[0]), ffn_w_gu[0].astype(BF16), ffn_w_down[0].astype(BF16))
    h = _conv_module(h, row(norm_mix_g[1]), conv_w_pw1[0].astype(BF16), row(conv_b_pw1[0]),
                     conv_w_dw[0], row(conv_b_dw[0]), row(conv_ln_g[0]), row(conv_ln_b[0]),
                     conv_w_pw2[0].astype(BF16), row(conv_b_pw2[0]), B, S)
    out = _moe_layer(h, row(norm_ffn_g[1]), row(final_norm_g), moe_w_router[0],
                     moe_w_gu[0].astype(BF16), moe_w_down[0].astype(BF16))
    return out.reshape(B, S, D)
```

```python
import functools

import numpy as np
import jax
import jax.numpy as jnp
from jax import lax
from jax.experimental import pallas as pl
from jax.experimental.pallas import tpu as pltpu
from jax.experimental.pallas import tpu_sc as plsc

F32 = jnp.float32
BF16 = jnp.bfloat16

EPS = 1e-6
ROPE_BASE = 10000.0
HEADS = 4
QK_DIM = 256
V_DIM = 512
CONV_WIDTH = 31
N_EXPERTS = 8
LANES = 128
SUBLANES = 8
HALO = 32
VMEM_LIMIT = 56 * 1024 * 1024

RET_CHUNK = 256
FF_CHUNK = 256
MOE_TILE = 1024
SC_CHUNK_BYTES = 128 * 1024


def _cparams(sem):
    return pltpu.CompilerParams(dimension_semantics=sem, vmem_limit_bytes=VMEM_LIMIT)


def _resident(shape):
    zeros = (0,) * len(shape)
    return pl.BlockSpec(shape, lambda *_: zeros, pipeline_mode=pl.Buffered(1))


def _rms(h, g):
    return h * lax.rsqrt(jnp.mean(h * h, axis=-1, keepdims=True) + EPS) * g


def _silu(x):
    return x * jax.nn.sigmoid(x)


def _pack_pairs(x):
    bits = lax.bitcast_convert_type(x.astype(BF16).astype(F32), jnp.uint32)
    half = x.shape[1] // 2
    words = lax.shift_right_logical(bits[:, :half], jnp.uint32(16)) | (bits[:, half:] & jnp.uint32(0xFFFF0000))
    return lax.bitcast_convert_type(words, jnp.int32)


def _unpack_pairs(words):
    w = lax.bitcast_convert_type(words, jnp.uint32)
    lo = lax.bitcast_convert_type(lax.shift_left(w, jnp.uint32(16)), F32)
    hi = lax.bitcast_convert_type(w & jnp.uint32(0xFFFF0000), F32)
    return lo, hi


def _inproj_kernel(h_ref, g_ref, pos_ref, inv_ref, w_ref, o_ref):
    hn = _rms(h_ref[...], g_ref[...]).astype(BF16)
    ang = pos_ref[...].astype(F32) * inv_ref[...]
    c = jnp.cos(ang)
    s = jnp.sin(ang)
    half = QK_DIM // 2
    qk_w = HEADS * QK_DIM
    for hd in range(2 * HEADS):
        col = hd * QK_DIM
        a = jnp.dot(hn, w_ref[:, col:col + QK_DIM], preferred_element_type=F32)
        t1 = a[:, :half]
        t2 = a[:, half:]
        scale = 1.0 if hd < HEADS else QK_DIM ** -0.5
        o_ref[:, col:col + half] = ((t1 * c - t2 * s) * scale).astype(BF16)
        o_ref[:, col + half:col + QK_DIM] = ((t2 * c + t1 * s) * scale).astype(BF16)
    blk = 1024
    for col in range(2 * qk_w, w_ref.shape[1], blk):
        o_ref[:, col:col + blk] = jnp.dot(hn, w_ref[:, col:col + blk], preferred_element_type=F32).astype(BF16)


def _inproj(h, g, posb, inv, w, tm=1024):
    T, D = h.shape
    N = w.shape[1]
    return pl.pallas_call(
        _inproj_kernel,
        grid=(T // tm,),
        in_specs=[
            pl.BlockSpec((tm, D), lambda i: (i, 0)),
            _resident((1, D)),
            pl.BlockSpec((tm, LANES), lambda i: (i, 0)),
            _resident((1, LANES)),
            _resident((D, N)),
        ],
        out_specs=pl.BlockSpec((tm, N), lambda i: (i, 0)),
        out_shape=jax.ShapeDtypeStruct((T, N), BF16),
        compiler_params=_cparams(("parallel",)),
        name="ret_inproj",
    )(h, g, posb, inv, w)


def _retention_kernel(gamma_c, q_ref, k_ref, v_ref, g_ref, dec_ref, xi_ref, zeta_ref, wo_ref, h_ref,
                      o_ref, st_ref, gated_ref):
    n = pl.program_id(1)

    @pl.when(n == 0)
    def _():
        st_ref[...] = jnp.zeros_like(st_ref)

    for h in range(HEADS):
        q = q_ref[:, h * QK_DIM:(h + 1) * QK_DIM]
        k = k_ref[:, h * QK_DIM:(h + 1) * QK_DIM]
        v = v_ref[:, h * V_DIM:(h + 1) * V_DIM]
        s = lax.dot_general(q, k, (((1,), (1,)), ((), ())), preferred_element_type=F32)
        s = s * dec_ref[h]
        intra = jnp.dot(s.astype(BF16), v, preferred_element_type=F32)
        state = st_ref[h]
        cross = jnp.dot(q, state.astype(BF16), preferred_element_type=F32) * xi_ref[h]
        o = intra + cross
        kz = (k.astype(F32) * zeta_ref[h]).astype(BF16)
        st_ref[h] = state * gamma_c[h] + lax.dot_general(
            kz, v, (((0,), (0,)), ((), ())), preferred_element_type=F32)
        mu = jnp.mean(o, axis=-1, keepdims=True)
        d = o - mu
        var = jnp.mean(d * d, axis=-1, keepdims=True)
        on = d * lax.rsqrt(var + EPS)
        gate = g_ref[:, h * V_DIM:(h + 1) * V_DIM].astype(F32)
        gated_ref[:, h * V_DIM:(h + 1) * V_DIM] = (_silu(gate) * on).astype(BF16)

    o_ref[...] = h_ref[...] + jnp.dot(gated_ref[...], wo_ref[...], preferred_element_type=F32)


def _retention_consts(C):
    log_gamma = np.log1p(-(2.0 ** (-5.0 - np.arange(HEADS, dtype=np.float64))))
    idx = np.arange(C, dtype=np.float64)
    rel = idx[:, None] - idx[None, :]
    decay = np.where(rel[None] >= 0, np.exp(np.maximum(rel, 0.0)[None] * log_gamma[:, None, None]), 0.0)
    xi = np.exp((idx + 1.0)[None, :] * log_gamma[:, None])
    zeta = np.exp((C - 1.0 - idx)[None, :] * log_gamma[:, None])
    gamma_c = tuple(float(x) for x in np.exp(C * log_gamma))
    xi_b = np.broadcast_to(xi[:, :, None], (HEADS, C, V_DIM))
    zeta_b = np.broadcast_to(zeta[:, :, None], (HEADS, C, QK_DIM))
    return (gamma_c, jnp.asarray(decay, F32), jnp.asarray(xi_b, F32), jnp.asarray(zeta_b, F32))


def _retention(proj, w_out, h, B, S):
    T, D = h.shape
    C = RET_CHUNK
    N = S // C
    qk_w = HEADS * QK_DIM
    v_w = HEADS * V_DIM
    gamma_c, decay, xi_b, zeta_b = _retention_consts(C)
    row = lambda b, n: b * N + n
    return pl.pallas_call(
        functools.partial(_retention_kernel, gamma_c),
        grid=(B, N),
        in_specs=[
            pl.BlockSpec((C, qk_w), lambda b, n: (row(b, n), 0)),
            pl.BlockSpec((C, qk_w), lambda b, n: (row(b, n), 1)),
            pl.BlockSpec((C, v_w), lambda b, n: (row(b, n), 1)),
            pl.BlockSpec((C, v_w), lambda b, n: (row(b, n), 2)),
            _resident((HEADS, C, C)),
            _resident((HEADS, C, V_DIM)),
            _resident((HEADS, C, QK_DIM)),
            _resident((v_w, D)),
            pl.BlockSpec((C, D), lambda b, n: (row(b, n), 0)),
        ],
        out_specs=pl.BlockSpec((C, D), lambda b, n: (row(b, n), 0)),
        out_shape=jax.ShapeDtypeStruct((T, D), F32),
        scratch_shapes=[pltpu.VMEM((HEADS, QK_DIM, V_DIM), F32), pltpu.VMEM((C, v_w), BF16)],
        compiler_params=_cparams(("parallel", "arbitrary")),
        name="retention",
    )(proj, proj, proj, proj, decay, xi_b, zeta_b, w_out, h)


def _ffn_kernel(h_ref, g_ref, wgu_ref, wd_ref, o_ref):
    h = h_ref[...]
    hn = _rms(h, g_ref[...]).astype(BF16)
    F = wd_ref.shape[0]
    acc = jnp.zeros(h.shape, F32)
    for c0 in range(0, F, FF_CHUNK):
        gate = jnp.dot(hn, wgu_ref[:, c0:c0 + FF_CHUNK], preferred_element_type=F32)
        up = jnp.dot(hn, wgu_ref[:, F + c0:F + c0 + FF_CHUNK], preferred_element_type=F32)
        act = (_silu(gate) * up).astype(BF16)
        acc = acc + jnp.dot(act, wd_ref[c0:c0 + FF_CHUNK, :], preferred_element_type=F32)
    o_ref[...] = h + acc


def _ffn(h, g, w_gu, w_down, tm=512):
    T, D = h.shape
    F = w_down.shape[0]
    return pl.pallas_call(
        _ffn_kernel,
        grid=(T // tm,),
        in_specs=[
            pl.BlockSpec((tm, D), lambda i: (i, 0)),
            _resident((1, D)),
            _resident((D, 2 * F)),
            _resident((F, D)),
        ],
        out_specs=pl.BlockSpec((tm, D), lambda i: (i, 0)),
        out_shape=jax.ShapeDtypeStruct((T, D), F32),
        compiler_params=_cparams(("parallel",)),
        name="dense_swiglu",
    )(h, g, w_gu, w_down)


def _conv_kernel(tiles_per_seq, hc_ref, hn_ref, g_ref, w1_ref, b1_ref, wdw_ref, bdw_ref, lng_ref, lnb_ref,
                 w2_ref, b2_ref, o_ref, u_ref, us_ref, c_ref):
    t = pl.program_id(0)
    ts, D = hc_ref.shape
    cur = lax.rem(t, 2)
    prev = 1 - cur

    @pl.when(t == 0)
    def _():
        u_ref[...] = jnp.zeros_like(u_ref)

    L = ts + HALO - SUBLANES
    us_ref[0] = u_ref[prev]
    for k in range(1, SUBLANES):
        us_ref[k, 0:L, :] = u_ref[prev, k:k + L, :]

    rc = 32
    cw = D // 2
    base = HALO - (CONV_WIDTH - 1)
    for r0 in range(0, ts, rc):
        for c0 in range(0, D, cw):
            acc = jnp.zeros((rc, cw), F32)
            for j in range(CONV_WIDTH):
                al, ph = divmod(base + j, SUBLANES)
                row0 = r0 + al * SUBLANES
                acc = acc + us_ref[ph, row0:row0 + rc, c0:c0 + cw] * wdw_ref[j:j + 1, c0:c0 + cw]
            c_ref[r0:r0 + rc, c0:c0 + cw] = acc

    c = c_ref[...] + bdw_ref[...]
    mu = jnp.mean(c, axis=-1, keepdims=True)
    d = c - mu
    var = jnp.mean(d * d, axis=-1, keepdims=True)
    un = d * lax.rsqrt(var + EPS) * lng_ref[...] + lnb_ref[...]
    act = _silu(un).astype(BF16)
    o_ref[...] = hc_ref[...] + jnp.dot(act, w2_ref[...], preferred_element_type=F32) + b2_ref[...]

    first = lax.rem(t, tiles_per_seq) == 0
    u_ref[cur, 0:HALO, :] = jnp.where(first, 0.0, u_ref[prev, ts:ts + HALO, :])
    hn = _rms(hn_ref[...], g_ref[...]).astype(BF16)
    a = jnp.dot(hn, w1_ref[...], preferred_element_type=F32) + b1_ref[...]
    u_ref[cur, HALO:HALO + ts, :] = a[:, :D] * jax.nn.sigmoid(a[:, D:])


def _conv_module(h, g, w1, b1, wdw, bdw, lng, lnb, w2, b2, B, S, ts=512):
    T, D = h.shape
    nt = T // ts
    W = wdw.shape[0]
    conv_tile = lambda t: (jnp.maximum(t - 1, 0), 0)
    glu_tile = lambda t: (jnp.minimum(t, nt - 1), 0)
    return pl.pallas_call(
        functools.partial(_conv_kernel, S // ts),
        grid=(nt + 1,),
        in_specs=[
            pl.BlockSpec((ts, D), conv_tile),
            pl.BlockSpec((ts, D), glu_tile),
            _resident((1, D)),
            _resident((D, 2 * D)),
            _resident((1, 2 * D)),
            _resident((W, D)),
            _resident((1, D)),
            _resident((1, D)),
            _resident((1, D)),
            _resident((D, D)),
            _resident((1, D)),
        ],
        out_specs=pl.BlockSpec((ts, D), conv_tile),
        out_shape=jax.ShapeDtypeStruct((T, D), F32),
        scratch_shapes=[pltpu.VMEM((2, ts + HALO, D), F32), pltpu.VMEM((SUBLANES, ts + HALO, D), F32),
                        pltpu.VMEM((ts, D), F32)],
        compiler_params=_cparams(("arbitrary",)),
        name="conv_module",
    )(h, h, g, w1, b1, wdw, bdw, lng, lnb, w2, b2)


def _router_kernel(h_ref, g_ref, wr_ref, hn_ref, meta_ref, metat_ref, cnt_ref, run_ref):
    i = pl.program_id(0)
    tm = h_ref.shape[0]

    @pl.when(i == 0)
    def _():
        run_ref[...] = jnp.zeros_like(run_ref)

    hn = _rms(h_ref[...], g_ref[...])
    hi = hn.astype(BF16)
    lo = (hn - hi.astype(F32)).astype(BF16)
    hn_ref[...] = _pack_pairs(hn)
    w = wr_ref[...]
    hw = jnp.dot(hi, w, preferred_element_type=F32)
    logits = hw[:, :LANES] + (hw[:, LANES:] + jnp.dot(lo, w[:, :LANES], preferred_element_type=F32))
    lane = lax.broadcasted_iota(jnp.int32, logits.shape, 1)
    neg = jnp.float32(-jnp.inf)
    logits = jnp.where(lane < N_EXPERTS, logits, neg)
    m1 = jnp.max(logits, axis=-1, keepdims=True)
    i1 = jnp.min(jnp.where(logits == m1, lane, LANES), axis=-1, keepdims=True)
    rest = jnp.where(lane == i1, neg, logits)
    m2 = jnp.max(rest, axis=-1, keepdims=True)
    i2 = jnp.min(jnp.where(rest == m2, lane, LANES), axis=-1, keepdims=True)
    e = jnp.exp(m2 - m1)
    g1 = 1.0 / (1.0 + e)
    g2 = e / (1.0 + e)

    oh1 = (lane == i1).astype(F32)
    oh2 = (lane == i2).astype(F32)
    sel = oh1 + oh2
    r_i = lax.broadcasted_iota(jnp.int32, (tm, tm), 0)
    c_i = lax.broadcasted_iota(jnp.int32, (tm, tm), 1)
    tri = (c_i < r_i).astype(BF16)
    prefix = jnp.dot(tri, sel.astype(BF16), preferred_element_type=F32) + run_ref[...]
    rank1 = jnp.sum(oh1 * prefix, axis=-1, keepdims=True)
    rank2 = jnp.sum(oh2 * prefix, axis=-1, keepdims=True)
    run_ref[...] = run_ref[...] + jnp.sum(sel, axis=0, keepdims=True)
    cnt_ref[...] = jnp.broadcast_to(run_ref[...], cnt_ref.shape)

    meta = jnp.where(lane == 0, i1.astype(F32), 0.0)
    meta = jnp.where(lane == 1, i2.astype(F32), meta)
    meta = jnp.where(lane == 2, rank1, meta)
    meta = jnp.where(lane == 3, rank2, meta)
    meta = jnp.where(lane == 4, g1, meta)
    meta = jnp.where(lane == 5, g2, meta)
    meta_ref[...] = meta
    metat_ref[...] = meta.T[:SUBLANES, :]


def _router(h, g, wr_pad, tm=512):
    T, D = h.shape
    return pl.pallas_call(
        _router_kernel,
        grid=(T // tm,),
        in_specs=[
            pl.BlockSpec((tm, D), lambda i: (i, 0)),
            _resident((1, D)),
            _resident((D, 2 * LANES)),
        ],
        out_specs=[
            pl.BlockSpec((tm, D // 2), lambda i: (i, 0)),
            pl.BlockSpec((tm, LANES), lambda i: (i, 0)),
            pl.BlockSpec((SUBLANES, tm), lambda i: (0, i)),
            pl.BlockSpec((SUBLANES, LANES), lambda i: (0, 0)),
        ],
        out_shape=[
            jax.ShapeDtypeStruct((T, D // 2), jnp.int32),
            jax.ShapeDtypeStruct((T, LANES), F32),
            jax.ShapeDtypeStruct((SUBLANES, T), F32),
            jax.ShapeDtypeStruct((SUBLANES, LANES), F32),
        ],
        scratch_shapes=[pltpu.VMEM((1, LANES), F32)],
        compiler_params=_cparams(("arbitrary",)),
        name="moe_router",
    )(h, g, wr_pad)


def _sc_gather(table, idx):
    D = table.shape[1]
    B = idx.shape[0]
    mesh = plsc.VectorSubcoreMesh(core_axis_name="c", subcore_axis_name="s")
    nw = mesh.num_cores * mesh.num_subcores
    k = SC_CHUNK_BYTES // (D * table.dtype.itemsize)
    b_per_w = B // nw
    n_pairs = b_per_w // (2 * k)
    assert B % (nw * 2 * k) == 0, (B, nw, k)

    @functools.partial(
        pl.kernel, mesh=mesh,
        out_type=jax.ShapeDtypeStruct((B, D), table.dtype),
        scratch_types=[
            pltpu.VMEM((b_per_w,), jnp.int32),
            pltpu.VMEM((k, D), table.dtype),
            pltpu.VMEM((k, D), table.dtype),
            pltpu.SemaphoreType.DMA,
            pltpu.SemaphoreType.DMA,
        ],
    )
    def gather_kernel(table_hbm, idx_hbm, out_hbm, idx_v, buf0, buf1, sem0, sem1):
        wid = lax.axis_index("s") * mesh.num_cores + lax.axis_index("c")
        base = pl.multiple_of(wid * b_per_w, SUBLANES)
        pltpu.sync_copy(idx_hbm.at[pl.ds(base, b_per_w)], idx_v)

        def gather(c, buf, sem):
            rows = idx_v.at[pl.ds(pl.multiple_of(c * k, SUBLANES), k)]
            return pltpu.make_async_copy(table_hbm.at[rows], buf, sem)

        def store(c, buf):
            pltpu.sync_copy(buf, out_hbm.at[pl.ds(pl.multiple_of(base + c * k, SUBLANES), k)])

        gather(0, buf0, sem0).start()

        @pl.loop(0, n_pairs)
        def _(p):
            c = 2 * p
            gather(c + 1, buf1, sem1).start()
            gather(c, buf0, sem0).wait()
            store(c, buf0)

            @pl.when(p + 1 < n_pairs)
            def _():
                gather(c + 2, buf0, sem0).start()

            gather(c + 1, buf1, sem1).wait()
            store(c + 1, buf1)

    return gather_kernel(table, idx)


def _moe_ffn_kernel(te_ref, nv_ref, x_ref, wgu_ref, wd_ref, o_ref):
    del te_ref
    i = pl.program_id(0)
    live = nv_ref[i] > 0

    @pl.when(live)
    def _():
        lo, hi = _unpack_pairs(x_ref[...])
        x = jnp.concatenate([lo.astype(BF16), hi.astype(BF16)], axis=1)
        F = wd_ref.shape[1]
        acc = jnp.zeros((x.shape[0], wd_ref.shape[2]), F32)
        for c0 in range(0, F, FF_CHUNK):
            gate = jnp.dot(x, wgu_ref[0, :, c0:c0 + FF_CHUNK], preferred_element_type=F32)
            up = jnp.dot(x, wgu_ref[0, :, F + c0:F + c0 + FF_CHUNK], preferred_element_type=F32)
            act = (_silu(gate) * up).astype(BF16)
            acc = acc + jnp.dot(act, wd_ref[0, c0:c0 + FF_CHUNK, :], preferred_element_type=F32)
        o_ref[...] = _pack_pairs(acc)

    @pl.when(jnp.logical_not(live))
    def _():
        o_ref[...] = jnp.zeros_like(o_ref)


def _moe_ffn(tile_expert, tile_rows, xs, w_gu, w_down, tm):
    R = xs.shape[0]
    _, F, D = w_down.shape
    grid_spec = pltpu.PrefetchScalarGridSpec(
        num_scalar_prefetch=2,
        grid=(R // tm,),
        in_specs=[
            pl.BlockSpec((tm, D // 2), lambda i, te, nv: (i, 0)),
            pl.BlockSpec((1, D, 2 * F), lambda i, te, nv: (te[i], 0, 0), pipeline_mode=pl.Buffered(1)),
            pl.BlockSpec((1, F, D), lambda i, te, nv: (te[i], 0, 0), pipeline_mode=pl.Buffered(1)),
        ],
        out_specs=pl.BlockSpec((tm, D // 2), lambda i, te, nv: (i, 0)),
    )
    return pl.pallas_call(
        _moe_ffn_kernel,
        grid_spec=grid_spec,
        out_shape=jax.ShapeDtypeStruct((R, D // 2), jnp.int32),
        compiler_params=_cparams(("arbitrary",)),
        name="moe_swiglu",
    )(tile_expert, tile_rows, xs, w_gu, w_down)


def _combine_kernel(h_ref, meta_ref, g_ref, y1_ref, y2_ref, o_ref):
    meta = meta_ref[...]
    g1 = meta[:, 4:5]
    g2 = meta[:, 5:6]
    lo1, hi1 = _unpack_pairs(y1_ref[...])
    lo2, hi2 = _unpack_pairs(y2_ref[...])
    half = lo1.shape[1]
    h = h_ref[...]
    h = jnp.concatenate([h[:, :half] + g1 * lo1 + g2 * lo2, h[:, half:] + g1 * hi1 + g2 * hi2], axis=1)
    o_ref[...] = _rms(h, g_ref[...])


def _combine(h, meta, g, y, tb=1024):
    T, D = h.shape
    nb = T // tb
    return pl.pallas_call(
        _combine_kernel,
        grid=(nb,),
        in_specs=[
            pl.BlockSpec((tb, D), lambda i: (i, 0)),
            pl.BlockSpec((tb, LANES), lambda i: (i, 0)),
            _resident((1, D)),
            pl.BlockSpec((tb, D // 2), lambda i: (i, 0)),
            pl.BlockSpec((tb, D // 2), lambda i: (i + nb, 0)),
        ],
        out_specs=pl.BlockSpec((tb, D), lambda i: (i, 0)),
        out_shape=jax.ShapeDtypeStruct((T, D), F32),
        compiler_params=_cparams(("parallel",)),
        name="moe_combine",
    )(h, meta, g, y, y)


def _moe_layer(h, g_ffn, g_final, w_router, w_gu, w_down, tm=MOE_TILE):
    T, D = h.shape
    E = w_router.shape[1]
    wr_pad = jnp.zeros((D, LANES), F32).at[:, :E].set(w_router)
    wr_hi = wr_pad.astype(BF16)
    wr_lo = (wr_pad - wr_hi.astype(F32)).astype(BF16)
    hn, meta, meta_t, cnt = _router(h, g_ffn, jnp.concatenate([wr_hi, wr_lo], axis=1))

    counts = cnt[0, :E].astype(jnp.int32)
    padded = ((counts + tm - 1) // tm) * tm
    ends = jnp.cumsum(padded)
    offsets = ends - padded
    route = meta_t.astype(jnp.int32)
    pos = jnp.concatenate([offsets[route[0]] + route[2], offsets[route[1]] + route[3]])
    n_tiles = -(-(2 * T + E * (tm - 1)) // tm)
    R = n_tiles * tm
    tok = jnp.arange(T, dtype=jnp.int32)
    _, tok_sorted = lax.sort_key_val(pos, jnp.concatenate([tok, tok]))
    tok_sorted = jnp.concatenate([tok_sorted, jnp.zeros((T,), jnp.int32)])
    starts = jnp.cumsum(counts) - counts
    src = jnp.zeros((R + T,), jnp.int32)
    for e in range(E):
        src = lax.dynamic_update_slice(src, lax.dynamic_slice(tok_sorted, (starts[e],), (T,)), (offsets[e],))
    src = src[:R]
    tile_start = jnp.arange(n_tiles, dtype=jnp.int32) * tm
    tile_expert = jnp.minimum(jnp.sum(tile_start[:, None] >= ends[None, :], axis=1), E - 1).astype(jnp.int32)
    tile_rows = jnp.clip((offsets + counts)[tile_expert] - tile_start, 0, tm).astype(jnp.int32)

    xs = _sc_gather(hn, src)
    ys = _moe_ffn(tile_expert, tile_rows, xs, w_gu, w_down, tm)
    y = _sc_gather(ys, pos)
    return _combine(h, meta, g_final, y)


def kernel(x, positions, norm_mix_g, norm_ffn_g, final_norm_g, ret_w_in, ret_w_out, conv_w_pw1, conv_b_pw1, conv_w_dw, conv_b_dw, conv_ln_g, conv_ln_b, conv_w_pw2, conv_b_pw2, ffn_w_gu, ffn_w_down, moe_w_router, moe_w_gu, moe_w_down):
    B, S, D = x.shape
    T = B * S
    h = x.reshape(T, D)
    row = lambda v: v.reshape(1, -1)

    posb = jnp.broadcast_to(positions.reshape(T, 1), (T, LANES))
    half = QK_DIM // 2
    inv = (ROPE_BASE ** (-jnp.arange(half, dtype=F32) / half)).reshape(1, half)
    proj = _inproj(h, row(norm_mix_g[0]), posb, inv, ret_w_in[0].astype(BF16))
    h = _retention(proj, ret_w_out[0].astype(BF16), h, B, S)
    h = _ffn(h, row(norm_ffn_g[0]), ffn_w_gu[0].astype(BF16), ffn_w_down[0].astype(BF16))
    h = _conv_module(h, row(norm_mix_g[1]), conv_w_pw1[0].astype(BF16), row(conv_b_pw1[0]),
                     conv_w_dw[0], row(conv_b_dw[0]), row(conv_ln_g[0]), row(conv_ln_b[0]),
                     conv_w_pw2[0].astype(BF16), row(conv_b_pw2[0]), B, S)
    out = _moe_layer(h, row(norm_ffn_g[1]), row(final_norm_g), moe_w_router[0],
                     moe_w_gu[0].astype(BF16), moe_w_down[0].astype(BF16))
    return out.reshape(B, S, D)
```
